```python
import jax
import jax.numpy as jnp
from jax import lax
import numpy as np

D_MODEL = 1024
BATCH = 4
SEQ = 8192
DEPTH = 4
DEC_BATCH = 8
DEC_SEQ = 64
PAST_LEN = 1024

CHUNK = 64
D_MIX = D_MODEL
H_A = 4
DV_A = D_MIX // 2 // H_A
DK_A = DV_A // 2
R_A = 16
TAU_A = 16.0
D_B = D_MIX // 2
W_B = 31
D_C = D_MIX // 2
H_C = 8
DH_C = D_C // H_C
LRU_C = 8.0
H_D = 4
DK_D = 128
DV_D = D_MIX // 2 // H_D
W_S = 4
D_FF = 2688
W_F = 3
N_EVEN = (DEPTH + 1) // 2
N_ODD = DEPTH // 2
ALPHA = (2 * DEPTH) ** 0.25
BETA = (8 * DEPTH) ** -0.25
EPS = 1e-5
EVEN_SPLITS = (H_A * DK_A, H_A * DK_A, H_A * DV_A, H_A * DV_A, R_A, 2 * D_B)
E_IN = sum(EVEN_SPLITS)
CONV_ODD = D_C + 2 * H_D * DK_D + H_D * DV_D
ODD_SPLITS = (CONV_ODD, D_C, H_D * DV_D, H_D, H_D)
O_IN = sum(ODD_SPLITS)

kernel_name = 'hybrid_streaming_encoder_step'


def _split(t, sizes):
    out, start = [], 0
    for s in sizes:
        out.append(t[..., start:start + s])
        start += s
    return out


def _layernorm(x, g, b):
    xf = x.astype(jnp.float32)
    mu = jnp.mean(xf, -1, keepdims=True)
    var = jnp.mean(jnp.square(xf - mu), -1, keepdims=True)
    return ((xf - mu) * lax.rsqrt(var + EPS) * g + b).astype(x.dtype)


def _rmsnorm_heads(x, g):
    xf = x.astype(jnp.float32)
    y = xf * lax.rsqrt(jnp.mean(jnp.square(xf), -1, keepdims=True) + EPS)
    return y.reshape(y.shape[:-2] + (-1,)) * g


def _l2norm(t):
    return t * lax.rsqrt(jnp.sum(t * t, -1, keepdims=True) + 1e-6)


def _causal_dwconv(x, buf, w, b):
    xp = jnp.concatenate([buf.astype(x.dtype), x], axis=1)
    y = lax.conv_general_dilated(xp, w[:, None, :].astype(x.dtype), window_strides=(1,), padding='VALID',
                                 dimension_numbers=('NWC', 'WIO', 'NWC'), feature_group_count=x.shape[-1])
    return y + b.astype(y.dtype), xp[:, -(w.shape[0] - 1):]


def _to_chunks(t, n, c):
    return jnp.moveaxis(t.reshape(t.shape[:2] + (n, c) + t.shape[3:]), 2, 0)


def _from_chunks(t):
    t = jnp.moveaxis(t, 0, 2)
    return t.reshape(t.shape[:2] + (-1, t.shape[-1]))


def _gla(q, k, v, log_a, s0):
    L = q.shape[2]
    c = min(L, CHUNK)
    n = L // c
    causal = jnp.tril(jnp.ones((c, c), bool))

    def step(s, inp):
        qc, kc, vc, ac = inp
        bcum = jnp.cumsum(ac, axis=2)
        o_inter = jnp.einsum('bhtk,bhkv->bhtv', qc * jnp.exp(bcum), s)
        diff = bcum[:, :, :, None, :] - bcum[:, :, None, :, :]
        decay = jnp.exp(jnp.where(causal[:, :, None], diff, -jnp.inf))
        scores = jnp.einsum('bhtk,bhsk,bhtsk->bhts', qc, kc, decay)
        o = o_inter + jnp.einsum('bhts,bhsv->bhtv', scores, vc)
        blast = bcum[:, :, -1:, :]
        s_new = jnp.exp(blast[:, :, 0, :])[..., None] * s + jnp.einsum(
            'bhsk,bhsv->bhkv', kc * jnp.exp(blast - bcum), vc)
        return s_new, o

    s_fin, o = lax.scan(step, s0, (_to_chunks(q, n, c), _to_chunks(k, n, c), _to_chunks(v, n, c),
                                   _to_chunks(log_a, n, c)))
    return _from_chunks(o), s_fin


def _gated_delta(q, k, v, beta, g, s0):
    L = q.shape[2]
    c = min(L, CHUNK)
    n = L // c
    dv = v.shape[-1]
    causal = jnp.tril(jnp.ones((c, c), bool))
    strict = jnp.tril(jnp.ones((c, c), bool), -1)
    eye = jnp.eye(c, dtype=jnp.float32)

    def step(s, inp):
        qc, kc, vc, bc, gc = inp
        gcum = jnp.cumsum(gc, axis=-1)
        decay = jnp.exp(jnp.where(causal, gcum[..., :, None] - gcum[..., None, :], -jnp.inf))
        kb = kc * bc[..., None]
        lower = jnp.where(strict, jnp.einsum('bhtk,bhsk->bhts', kb, kc) * decay, 0.0)
        rhs = jnp.concatenate([vc * bc[..., None], kb * jnp.exp(gcum)[..., None]], axis=-1)
        sol = lax.linalg.triangular_solve(eye + lower, rhs, left_side=True, lower=True, unit_diagonal=True)
        u, w = sol[..., :dv], sol[..., dv:]
        v_new = u - jnp.einsum('bhtk,bhkv->bhtv', w, s)
        attn = jnp.where(causal, jnp.einsum('bhtk,bhsk->bhts', qc, kc) * decay, 0.0)
        o = jnp.einsum('bhtk,bhkv->bhtv', qc * jnp.exp(gcum)[..., None], s) + jnp.einsum(
            'bhts,bhsv->bhtv', attn, v_new)
        glast = gcum[..., -1:]
        s_new = jnp.exp(glast)[..., None] * s + jnp.einsum(
            'bhsk,bhsv->bhkv', kc * jnp.exp(glast - gcum)[..., None], v_new)
        return s_new, o

    s_fin, o = lax.scan(step, s0, (_to_chunks(q, n, c), _to_chunks(k, n, c), _to_chunks(v, n, c),
                                   _to_chunks(beta, n, c), _to_chunks(g, n, c)))
    return _from_chunks(o), s_fin


def _rglru(xc, r, i, log_a_base, h0):
    log_a = LRU_C * r * log_a_base
    a = jnp.exp(log_a)
    bx = jnp.sqrt(-jnp.expm1(2.0 * log_a)) * (i * xc)
    bx = bx.at[:, 0].add(a[:, 0] * h0)

    def combine(e1, e2):
        a1, b1 = e1
        a2, b2 = e2
        return a1 * a2, a2 * b1 + b2

    _, h = lax.associative_scan(combine, (a, bx), axis=1)
    return h, h[:, -1]


def _even_mixer(x, s_gla, buf_b, w_in, w_lr, b_lr, g_gla, w_dw, b_dw, g_cn, b_cn, w_out):
    f32 = jnp.float32
    bsz, L, _ = x.shape
    q, k, v, gate, lr, glu = _split(x @ w_in, EVEN_SPLITS)
    heads = lambda t, d: t.astype(f32).reshape(bsz, L, H_A, d).transpose(0, 2, 1, 3)
    log_a = jax.nn.log_sigmoid((lr @ w_lr).astype(f32) + b_lr) / TAU_A
    o_a, s_new = _gla(heads(q, DK_A) * DK_A ** -0.5, heads(k, DK_A), heads(v, DV_A), heads(log_a, DK_A),
                      s_gla.astype(f32))
    o_a = _rmsnorm_heads(o_a.transpose(0, 2, 1, 3), g_gla) * jax.nn.silu(gate.astype(f32))
    u = glu[..., :D_B] * jax.nn.sigmoid(glu[..., D_B:])
    cv, buf_new = _causal_dwconv(u, buf_b, w_dw, b_dw)
    o_b = jax.nn.silu(_layernorm(cv, g_cn, b_cn))
    y = jnp.concatenate([o_a.astype(x.dtype), o_b.astype(x.dtype)], axis=-1) @ w_out
    return y, s_new.astype(x.dtype), buf_new


def _odd_mixer(x, h_lru, s_delta, buf, w_in, w_conv, b_conv, w_rg, b_rg, w_ig, b_ig, lam, a_log, dt_bias,
               g_delta, w_out):
    f32 = jnp.float32
    bsz, L, _ = x.shape
    conv_in, gate_c, z, beta_raw, a_raw = _split(x @ w_in, ODD_SPLITS)
    cv, buf_new = _causal_dwconv(conv_in, buf, w_conv, b_conv)
    xc = cv[..., :D_C].astype(f32)
    q, k, v = _split(jax.nn.silu(cv[..., D_C:].astype(f32)), (H_D * DK_D, H_D * DK_D, H_D * DV_D))
    xb = xc.reshape(bsz, L, H_C, DH_C)
    r = jax.nn.sigmoid(jnp.einsum('blhi,hij->blhj', xb, w_rg).reshape(bsz, L, D_C) + b_rg)
    ig = jax.nn.sigmoid(jnp.einsum('blhi,hij->blhj', xb, w_ig).reshape(bsz, L, D_C) + b_ig)
    h, h_last = _rglru(xc, r, ig, jax.nn.log_sigmoid(lam.astype(f32)), h_lru.astype(f32))
    o_c = h * jax.nn.gelu(gate_c.astype(f32))
    heads = lambda t, d: t.reshape(bsz, L, H_D, d).transpose(0, 2, 1, 3)
    qh = _l2norm(heads(q, DK_D)) * DK_D ** -0.5
    kh = _l2norm(heads(k, DK_D))
    vh = heads(v, DV_D)
    beta = jax.nn.sigmoid(beta_raw.astype(f32)).transpose(0, 2, 1)
    g = (-jnp.exp(a_log.astype(f32)) * jax.nn.softplus(a_raw.astype(f32) + dt_bias)).transpose(0, 2, 1)
    o_d, s_new = _gated_delta(qh, kh, vh, beta, g, s_delta.astype(f32))
    o_d = _rmsnorm_heads(o_d.transpose(0, 2, 1, 3), g_delta) * jax.nn.silu(z.astype(f32))
    y = jnp.concatenate([o_c, o_d], axis=-1).astype(x.dtype) @ w_out
    return y, h_last.astype(x.dtype), s_new.astype(x.dtype), buf_new


def _conv_ffn(x, buf, w_up, w_dw, b_dw, w_down):
    hu = x @ w_up
    cv, buf_new = _causal_dwconv(hu[..., :D_FF], buf, w_dw, b_dw)
    return (jax.nn.gelu(cv) * hu[..., D_FF:]) @ w_down, buf_new


def _zero_states(bsz, dtype):
    states = []
    for l in range(DEPTH):
        if l % 2 == 0:
            states.append((jnp.zeros((bsz, H_A, DK_A, DV_A), dtype), jnp.zeros((bsz, W_B - 1, D_B), dtype),
                           jnp.zeros((bsz, W_F - 1, D_FF), dtype)))
        else:
            states.append((jnp.zeros((bsz, D_C), dtype), jnp.zeros((bsz, H_D, DK_D, DV_D), dtype),
                           jnp.zeros((bsz, W_S - 1, CONV_ODD), dtype), jnp.zeros((bsz, W_F - 1, D_FF), dtype)))
    return states


def _trunk(x, states, even_w, odd_w, ffn_w):
    new_states = []
    for l in range(DEPTH):
        st = states[l]
        if l % 2 == 0:
            y, *mix_new = _even_mixer(x, st[0], st[1], *[w[l // 2] for w in even_w])
        else:
            y, *mix_new = _odd_mixer(x, st[0], st[1], st[2], *[w[l // 2] for w in odd_w])
        w_up, w_fdw, b_fdw, w_down, ln1_g, ln1_b, ln2_g, ln2_b = [w[l] for w in ffn_w]
        x = _layernorm(ALPHA * x + y, ln1_g, ln1_b)
        f, ffn_buf = _conv_ffn(x, st[-1], w_up, w_fdw, b_fdw, w_down)
        x = _layernorm(ALPHA * x + f, ln2_g, ln2_b)
        new_states.append((*mix_new, ffn_buf))
    return x, new_states


def setup_inputs(seed: int = 0) -> dict:
    key = jax.random.key(seed)
    keys = iter(jax.random.split(key, 64))
    nrm = lambda shape, s: jax.random.normal(next(keys), shape, jnp.float32) * s
    uni = lambda shape, lo, hi: jax.random.uniform(next(keys), shape, jnp.float32, lo, hi)
    gain = lambda shape: 1.0 + nrm(shape, 0.02)
    inp = {}
    inp['x_prompt'] = nrm((BATCH, SEQ, D_MODEL), 1.0)
    inp['x_sample'] = nrm((DEC_BATCH, DEC_SEQ, D_MODEL), 1.0)
    for l in range(DEPTH):
        if l % 2 == 0:
            inp[f'state_l{l}_gla'] = nrm((DEC_BATCH, H_A, DK_A, DV_A), 0.1)
            inp[f'cache_l{l}_dwconv'] = nrm((DEC_BATCH, W_B - 1, D_B), 0.5)
        else:
            inp[f'state_l{l}_lru'] = nrm((DEC_BATCH, D_C), 0.5)
            inp[f'state_l{l}_delta'] = nrm((DEC_BATCH, H_D, DK_D, DV_D), 0.1)
            inp[f'cache_l{l}_conv'] = nrm((DEC_BATCH, W_S - 1, CONV_ODD), 1.0)
        inp[f'cache_l{l}_ffn'] = nrm((DEC_BATCH, W_F - 1, D_FF), 1.0)
    inp['we_in'] = nrm((N_EVEN, D_MODEL, E_IN), D_MODEL ** -0.5)
    inp['we_lr'] = nrm((N_EVEN, R_A, H_A * DK_A), R_A ** -0.5)
    inp['be_lr'] = nrm((N_EVEN, H_A * DK_A), 0.1)
    inp['ge_gla'] = gain((N_EVEN, H_A * DV_A))
    inp['we_dw'] = nrm((N_EVEN, W_B, D_B), W_B ** -0.5)
    inp['be_dw'] = nrm((N_EVEN, D_B), 0.02)
    inp['ge_cn'] = gain((N_EVEN, D_B))
    inp['be_cn'] = nrm((N_EVEN, D_B), 0.02)
    inp['we_out'] = nrm((N_EVEN, D_MIX, D_MODEL), D_MIX ** -0.5 * BETA)
    inp['wo_in'] = nrm((N_ODD, D_MODEL, O_IN), D_MODEL ** -0.5)
    inp['wo_conv'] = nrm((N_ODD, W_S, CONV_ODD), W_S ** -0.5)
    inp['bo_conv'] = nrm((N_ODD, CONV_ODD), 0.02)
    inp['wo_rg'] = nrm((N_ODD, H_C, DH_C, DH_C), DH_C ** -0.5)
    inp['bo_rg'] = nrm((N_ODD, D_C), 0.02)
    inp['wo_ig'] = nrm((N_ODD, H_C, DH_C, DH_C), DH_C ** -0.5)
    inp['bo_ig'] = nrm((N_ODD, D_C), 0.02)
    a_pow = uni((N_ODD, D_C), 0.9, 0.999) ** (1.0 / LRU_C)
    inp['lam_lru'] = jnp.log(a_pow) - jnp.log1p(-a_pow)
    inp['a_log'] = jnp.log(uni((N_ODD, H_D), 1.0, 16.0))
    dt = jnp.exp(uni((N_ODD, H_D), float(np.log(1e-3)), float(np.log(1e-1))))
    inp['dt_bias'] = dt + jnp.log(-jnp.expm1(-dt))
    inp['go_delta'] = gain((N_ODD, H_D * DV_D))
    inp['wo_out'] = nrm((N_ODD, D_MIX, D_MODEL), D_MIX ** -0.5 * BETA)
    inp['w_up'] = nrm((DEPTH, D_MODEL, 2 * D_FF), D_MODEL ** -0.5)
    inp['w_fdw'] = nrm((DEPTH, W_F, D_FF), W_F ** -0.5)
    inp['b_fdw'] = nrm((DEPTH, D_FF), 0.02)
    inp['w_down'] = nrm((DEPTH, D_FF, D_MODEL), D_FF ** -0.5 * BETA)
    inp['ln1_g'] = gain((DEPTH, D_MODEL))
    inp['ln1_b'] = nrm((DEPTH, D_MODEL), 0.02)
    inp['ln2_g'] = gain((DEPTH, D_MODEL))
    inp['ln2_b'] = nrm((DEPTH, D_MODEL), 0.02)
    return inp


def reference(x_prompt, x_sample,
              state_l0_gla, cache_l0_dwconv, cache_l0_ffn,
              state_l1_lru, state_l1_delta, cache_l1_conv, cache_l1_ffn,
              state_l2_gla, cache_l2_dwconv, cache_l2_ffn,
              state_l3_lru, state_l3_delta, cache_l3_conv, cache_l3_ffn,
              we_in, we_lr, be_lr, ge_gla, we_dw, be_dw, ge_cn, be_cn, we_out,
              wo_in, wo_conv, bo_conv, wo_rg, bo_rg, wo_ig, bo_ig, lam_lru, a_log, dt_bias, go_delta, wo_out,
              w_up, w_fdw, b_fdw, w_down, ln1_g, ln1_b, ln2_g, ln2_b):
    even_w = (we_in, we_lr, be_lr, ge_gla, we_dw, be_dw, ge_cn, be_cn, we_out)
    odd_w = (wo_in, wo_conv, bo_conv, wo_rg, bo_rg, wo_ig, bo_ig, lam_lru, a_log, dt_bias, go_delta, wo_out)
    ffn_w = (w_up, w_fdw, b_fdw, w_down, ln1_g, ln1_b, ln2_g, ln2_b)
    y_prompt, new_p = _trunk(x_prompt, _zero_states(x_prompt.shape[0], x_prompt.dtype), even_w, odd_w, ffn_w)
    sample_states = [(state_l0_gla, cache_l0_dwconv, cache_l0_ffn),
                     (state_l1_lru, state_l1_delta, cache_l1_conv, cache_l1_ffn),
                     (state_l2_gla, cache_l2_dwconv, cache_l2_ffn),
                     (state_l3_lru, state_l3_delta, cache_l3_conv, cache_l3_ffn)]
    y_sample, new_s = _trunk(x_sample, sample_states, even_w, odd_w, ffn_w)
    (p0_gla, p0_dw, p0_ffn), (p1_lru, p1_delta, p1_conv, p1_ffn), (p2_gla, p2_dw, p2_ffn), \
        (p3_lru, p3_delta, p3_conv, p3_ffn) = new_p
    (s0_gla, s0_dw, s0_ffn), (s1_lru, s1_delta, s1_conv, s1_ffn), (s2_gla, s2_dw, s2_ffn), \
        (s3_lru, s3_delta, s3_conv, s3_ffn) = new_s
    return (y_prompt, y_sample,
            p0_gla, p0_dw, p0_ffn, p1_lru, p1_delta, p1_conv, p1_ffn,
            p2_gla, p2_dw, p2_ffn, p3_lru, p3_delta, p3_conv, p3_ffn,
            s0_gla, s0_dw, s0_ffn, s1_lru, s1_delta, s1_conv, s1_ffn,
            s2_gla, s2_dw, s2_ffn, s3_lru, s3_delta, s3_conv, s3_ffn)
```

```python
import functools

import numpy as np
import jax
import jax.numpy as jnp
from jax import lax
from jax.experimental import pallas as pl
from jax.experimental.pallas import tpu as pltpu

F32 = jnp.float32
BF16 = jnp.bfloat16

D_MODEL = 1024
DEPTH = 4
CHUNK = 64
H_A, DK_A, DV_A, R_A, TAU_A = 4, 64, 128, 16, 16.0
D_B, W_B = 512, 31
D_C, H_C, DH_C, LRU_C = 512, 8, 64, 8.0
H_D, DK_D, DV_D, W_S = 4, 128, 128, 4
D_FF, W_F = 2688, 3
ALPHA = (2 * DEPTH) ** 0.25
EPS = 1e-5
CONV_ODD = D_C + 2 * H_D * DK_D + H_D * DV_D

LANE = 128
TIME_TILE = 256
VMEM_LIMIT = 56 * 1024 * 1024
FFN_COLS = 896
GLA_LEVELS = 6

E_Q, E_K, E_V, E_G, E_GA, E_GB, E_LR, E_END = 0, 256, 512, 1024, 1536, 2048, 2560, 2688
O_CONV, O_GC, O_Z, O_BA, O_END = 0, 2048, 2560, 3072, 3200


def _dot(a, b):
    return jnp.dot(a, b, preferred_element_type=F32)


def _dot_nt(a, b):
    return lax.dot_general(a, b, (((1,), (1,)), ((), ())), preferred_element_type=F32)


def _dot_tn(a, b):
    return lax.dot_general(a, b, (((0,), (0,)), ((), ())), preferred_element_type=F32)


def _split_dot(m, x):
    hi = x.astype(BF16)
    lo = (x - hi.astype(F32)).astype(BF16)
    return _dot(m, hi) + _dot(m, lo)


def _sigmoid(x):
    return jax.nn.sigmoid(x)


def _silu(x):
    return x * jax.nn.sigmoid(x)


def _softplus(x):
    return jnp.maximum(x, 0.0) + jnp.log1p(jnp.exp(-jnp.abs(x)))


def _layernorm(x, g, b):
    mu = jnp.mean(x, axis=-1, keepdims=True)
    xc = x - mu
    var = jnp.mean(xc * xc, axis=-1, keepdims=True)
    return xc * lax.rsqrt(var + EPS) * g + b


def _rmsnorm(x):
    return x * lax.rsqrt(jnp.mean(x * x, axis=-1, keepdims=True) + EPS)


def _gla_constants():
    c = CHUNK
    r = np.arange(c)
    tri = (r[None, :] <= r[:, None]).astype(np.float32)
    rest = (r[None, :] > r[:, None]).astype(np.float32)
    mats = [tri, rest]
    masks = [np.eye(c, dtype=np.float32)]
    for lvl in range(GLA_LEVELS):
        m = (c // 2) >> lvl
        anchor = (r // (2 * m)) * (2 * m) + m - 1
        mats.append(tri - tri[anchor])
        same = (r[:, None] // (2 * m)) == (r[None, :] // (2 * m))
        masks.append((same & ((r[:, None] % (2 * m)) >= m) & ((r[None, :] % (2 * m)) < m)).astype(np.float32))
    return np.concatenate(mats, 0), np.stack(masks, 0)


def _even_kernel(x_ref, st0_ref, buf0_ref, win_ref, wlr_ref, blr_ref, ggla_ref, wdw_ref, bdw_ref, gcn_ref, bcn_ref,
                 wout_ref, lng_ref, lnb_ref, call_ref, masks_ref,
                 y_ref, stnew_ref, bufnew_ref,
                 proj_s, ubuf_s, state_s, cat_s, *, tm):
    t = pl.program_id(1)
    nt = pl.num_programs(1)

    @pl.when(t == 0)
    def _():
        state_s[...] = st0_ref[0]
        ubuf_s[0:32, :] = buf0_ref[0]

    x = x_ref[0]
    proj_s[...] = _dot(x.astype(BF16), win_ref[...])

    ubuf_s[32:32 + tm, :] = proj_s[:, E_GA:E_GB] * _sigmoid(proj_s[:, E_GB:E_LR])
    rb = 32
    for i in range(tm // rb):
        acc = jnp.broadcast_to(bdw_ref[...], (rb, D_B))
        for j in range(W_B):
            acc = acc + wdw_ref[j:j + 1, :] * ubuf_s[i * rb + 2 + j:i * rb + 2 + j + rb, :]
        ob = _silu(_layernorm(acc, gcn_ref[...], bcn_ref[...]))
        cat_s[i * rb:(i + 1) * rb, D_B:2 * D_B] = ob.astype(BF16)
    ubuf_s[0:32, :] = ubuf_s[tm:tm + 32, :]

    def chunk(c, carry):
        r0 = pl.multiple_of(c * CHUNK, CHUNK)
        rows = pl.ds(r0, CHUNK)
        q = proj_s[rows, E_Q:E_K] * (DK_A ** -0.5)
        k = proj_s[rows, E_K:E_V]
        z = _dot(proj_s[rows, E_LR:E_END].astype(BF16), wlr_ref[...]) + blr_ref[...]
        la = -_softplus(-z) * (1.0 / TAU_A)
        e_all = _split_dot(call_ref[...], la)
        bcum = e_all[0:CHUNK]
        qe = (q * jnp.exp(bcum)).astype(BF16)
        kr = (k * jnp.exp(e_all[CHUNK:2 * CHUNK])).astype(BF16)
        dlast = jnp.exp(bcum[CHUNK - 1:CHUNK, :])
        ql = [q.astype(BF16)]
        kl = [k.astype(BF16)]
        for lvl in range(GLA_LEVELS):
            f = jnp.exp(-jnp.abs(e_all[(2 + lvl) * CHUNK:(3 + lvl) * CHUNK]))
            ql.append((q * f).astype(BF16))
            kl.append((k * f).astype(BF16))
        for h in range(H_A):
            ks = slice(h * DK_A, (h + 1) * DK_A)
            vs = slice(E_V + h * DV_A, E_V + (h + 1) * DV_A)
            st = state_s[h]
            v = proj_s[rows, vs].astype(BF16)
            p = jnp.zeros((CHUNK, CHUNK), F32)
            for lvl in range(GLA_LEVELS + 1):
                p = p + masks_ref[lvl] * _dot_nt(ql[lvl][:, ks], kl[lvl][:, ks])
            o = _dot_nt(qe[:, ks], st.astype(BF16)) + _dot(p.astype(BF16), v)
            state_s[h] = st * dlast[:, ks] + _dot_tn(v, kr[:, ks])
            gate = proj_s[rows, E_G + h * DV_A:E_G + (h + 1) * DV_A]
            oa = _rmsnorm(o) * ggla_ref[:, h * DV_A:(h + 1) * DV_A] * _silu(gate)
            cat_s[rows, h * DV_A:(h + 1) * DV_A] = oa.astype(BF16)
        return carry

    lax.fori_loop(0, tm // CHUNK, chunk, 0)

    y = _dot(cat_s[...], wout_ref[...])
    y_ref[0] = _layernorm(ALPHA * x + y, lng_ref[...], lnb_ref[...])

    @pl.when(t == nt - 1)
    def _():
        stnew_ref[0] = state_s[...]
        bufnew_ref[0] = ubuf_s[0:32, :]


def _even_layer(x, st_t, buf32, w, tm):
    bsz, L, _ = x.shape
    nt = L // tm
    full = lambda a: pl.BlockSpec(a.shape, lambda b, t: (0,) * a.ndim)
    weights = (w['win'], w['wlr'], w['blr'], w['ggla'], w['wdw'], w['bdw'], w['gcn'], w['bcn'], w['wout'],
               w['lng'], w['lnb'], w['call'], w['masks'])
    return pl.pallas_call(
        functools.partial(_even_kernel, tm=tm),
        grid=(bsz, nt),
        in_specs=[pl.BlockSpec((1, tm, D_MODEL), lambda b, t: (b, t, 0)),
                  pl.BlockSpec((1, H_A, DV_A, DK_A), lambda b, t: (b, 0, 0, 0)),
                  pl.BlockSpec((1, 32, D_B), lambda b, t: (b, 0, 0))] + [full(a) for a in weights],
        out_specs=[pl.BlockSpec((1, tm, D_MODEL), lambda b, t: (b, t, 0)),
                   pl.BlockSpec((1, H_A, DV_A, DK_A), lambda b, t: (b, 0, 0, 0)),
                   pl.BlockSpec((1, 32, D_B), lambda b, t: (b, 0, 0))],
        out_shape=[jax.ShapeDtypeStruct((bsz, L, D_MODEL), F32),
                   jax.ShapeDtypeStruct((bsz, H_A, DV_A, DK_A), F32),
                   jax.ShapeDtypeStruct((bsz, 32, D_B), F32)],
        scratch_shapes=[pltpu.VMEM((tm, E_END), F32),
                        pltpu.VMEM((tm + 32, D_B), F32),
                        pltpu.VMEM((H_A, DV_A, DK_A), F32),
                        pltpu.VMEM((tm, D_MODEL), BF16)],
        compiler_params=pltpu.CompilerParams(dimension_semantics=("arbitrary", "arbitrary"),
                                             vmem_limit_bytes=VMEM_LIMIT),
        name="even_mixer",
    )(x, st_t, buf32, *weights)


def _odd_kernel(x_ref, h0_ref, st0_ref, buf0_ref, win_ref, wcv_ref, bcv_ref, wg_ref, bg_ref, lam_ref, alog_ref,
                dtb_ref, gdl_ref, wout_ref, lng_ref, lnb_ref, tri_ref,
                y_ref, hnew_ref, stnew_ref, bufnew_ref,
                cbuf_s, rest_s, cv_s, a_s, bx_s, hrow_s, state_s, cat_s, *, tm):
    t = pl.program_id(1)
    nt = pl.num_programs(1)

    @pl.when(t == 0)
    def _():
        hrow_s[...] = h0_ref[0]
        state_s[...] = st0_ref[0]
        cbuf_s[0:8, :] = buf0_ref[0]

    x = x_ref[0]
    xb = x.astype(BF16)
    cbuf_s[8:8 + tm, :] = _dot(xb, win_ref[:, O_CONV:O_GC])
    rest_s[...] = _dot(xb, win_ref[:, O_GC:O_END])

    rb = 32
    for i in range(tm // rb):
        for cb in range(CONV_ODD // D_C):
            cs = slice(cb * D_C, (cb + 1) * D_C)
            acc = jnp.broadcast_to(bcv_ref[:, cs], (rb, D_C))
            for j in range(W_S):
                acc = acc + wcv_ref[j:j + 1, cs] * cbuf_s[i * rb + 5 + j:i * rb + 5 + j + rb, cs]
            cv_s[i * rb:(i + 1) * rb, cs] = acc if cb == 0 else _silu(acc)
    cbuf_s[0:8, :] = cbuf_s[tm:tm + 8, :]

    xc = cv_s[:, 0:D_C]
    gates = _dot(xc.astype(BF16), wg_ref[...]) + bg_ref[...]
    log_a = LRU_C * _sigmoid(gates[:, 0:D_C]) * (-_softplus(-lam_ref[...]))
    a_s[...] = jnp.exp(log_a)
    one_m_a2 = -jnp.tanh(log_a) * (jnp.exp(2.0 * log_a) + 1.0)
    bx_s[...] = jnp.sqrt(one_m_a2) * (_sigmoid(gates[:, D_C:2 * D_C]) * xc)

    def lru_row(i, h):
        h = a_s[pl.ds(i, 1), :] * h + bx_s[pl.ds(i, 1), :]
        bx_s[pl.ds(i, 1), :] = h
        return h

    hrow_s[...] = lax.fori_loop(0, tm, lru_row, hrow_s[...])
    cat_s[:, 0:D_C] = (bx_s[...] * jax.nn.gelu(rest_s[:, 0:D_C])).astype(BF16)

    row = lax.broadcasted_iota(jnp.int32, (CHUNK, CHUNK), 0)
    col = lax.broadcasted_iota(jnp.int32, (CHUNK, CHUNK), 1)
    causal = row >= col
    strict = row > col

    def chunk(c, carry):
        r0 = pl.multiple_of(c * CHUNK, CHUNK)
        rows = pl.ds(r0, CHUNK)
        ba = rest_s[rows, O_BA - O_GC:O_END - O_GC]
        beta_all = _sigmoid(ba)
        g_all = -jnp.exp(alog_ref[...]) * _softplus(ba + dtb_ref[...])
        gcum_all = _split_dot(tri_ref[...], g_all)
        gcum_t = gcum_all.T
        for h in range(H_D):
            hs = slice(h * DK_D, (h + 1) * DK_D)
            qh = cv_s[rows, D_C + h * DK_D:D_C + (h + 1) * DK_D]
            kh = cv_s[rows, D_C + H_D * DK_D + h * DK_D:D_C + H_D * DK_D + (h + 1) * DK_D]
            vh = cv_s[rows, D_C + 2 * H_D * DK_D + h * DV_D:D_C + 2 * H_D * DK_D + (h + 1) * DV_D]
            qn = qh * lax.rsqrt(jnp.sum(qh * qh, axis=-1, keepdims=True) + 1e-6) * (DK_D ** -0.5)
            kn = kh * lax.rsqrt(jnp.sum(kh * kh, axis=-1, keepdims=True) + 1e-6)
            beta = beta_all[:, h:h + 1]
            gc = gcum_all[:, H_D + h:H_D + h + 1]
            gr = gcum_t[H_D + h:H_D + h + 1, :]
            glast = gc[CHUNK - 1:CHUNK, :]
            decay = jnp.where(causal, jnp.exp(gc - gr), 0.0)
            kb = kn * beta
            knb = kn.astype(BF16)
            n = -jnp.where(strict, _dot_nt(kb.astype(BF16), knb) * decay, 0.0)
            attn = _dot_nt(qn.astype(BF16), knb) * decay
            egc = jnp.exp(gc)
            sol = jnp.concatenate([vh * beta, kb * egc], axis=-1)
            nb = n.astype(BF16)
            for it in range(6):
                sol = sol + _dot(nb, sol.astype(BF16))
                if it < 5:
                    nb = _dot(nb, nb).astype(BF16)
            st = state_s[h]
            stb = st.astype(BF16)
            v_new = sol[:, 0:DV_D] - _dot(sol[:, DV_D:].astype(BF16), stb)
            vnb = v_new.astype(BF16)
            o = _dot((qn * egc).astype(BF16), stb) + _dot(attn.astype(BF16), vnb)
            state_s[h] = jnp.exp(glast) * st + _dot_tn((kn * jnp.exp(glast - gc)).astype(BF16), vnb)
            zg = rest_s[rows, O_Z - O_GC + h * DV_D:O_Z - O_GC + (h + 1) * DV_D]
            od = _rmsnorm(o) * gdl_ref[:, hs] * _silu(zg)
            cat_s[rows, D_C + h * DV_D:D_C + (h + 1) * DV_D] = od.astype(BF16)
        return carry

    lax.fori_loop(0, tm // CHUNK, chunk, 0)

    y = _dot(cat_s[...], wout_ref[...])
    y_ref[0] = _layernorm(ALPHA * x + y, lng_ref[...], lnb_ref[...])

    @pl.when(t == nt - 1)
    def _():
        hnew_ref[0] = hrow_s[...]
        stnew_ref[0] = state_s[...]
        bufnew_ref[0] = cbuf_s[0:8, :]


def _odd_layer(x, h0, st0, buf8, w, tm):
    bsz, L, _ = x.shape
    nt = L // tm
    full = lambda a: pl.BlockSpec(a.shape, lambda b, t: (0,) * a.ndim)
    weights = (w['win'], w['wcv'], w['bcv'], w['wg'], w['bg'], w['lam'], w['alog'], w['dtb'], w['gdl'], w['wout'],
               w['lng'], w['lnb'], w['tri'])
    return pl.pallas_call(
        functools.partial(_odd_kernel, tm=tm),
        grid=(bsz, nt),
        in_specs=[pl.BlockSpec((1, tm, D_MODEL), lambda b, t: (b, t, 0)),
                  pl.BlockSpec((1, 1, D_C), lambda b, t: (b, 0, 0)),
                  pl.BlockSpec((1, H_D, DK_D, DV_D), lambda b, t: (b, 0, 0, 0)),
                  pl.BlockSpec((1, 8, CONV_ODD), lambda b, t: (b, 0, 0))] + [full(a) for a in weights],
        out_specs=[pl.BlockSpec((1, tm, D_MODEL), lambda b, t: (b, t, 0)),
                   pl.BlockSpec((1, 1, D_C), lambda b, t: (b, 0, 0)),
                   pl.BlockSpec((1, H_D, DK_D, DV_D), lambda b, t: (b, 0, 0, 0)),
                   pl.BlockSpec((1, 8, CONV_ODD), lambda b, t: (b, 0, 0))],
        out_shape=[jax.ShapeDtypeStruct((bsz, L, D_MODEL), F32),
                   jax.ShapeDtypeStruct((bsz, 1, D_C), F32),
                   jax.ShapeDtypeStruct((bsz, H_D, DK_D, DV_D), F32),
                   jax.ShapeDtypeStruct((bsz, 8, CONV_ODD), F32)],
        scratch_shapes=[pltpu.VMEM((tm + 8, CONV_ODD), F32),
                        pltpu.VMEM((tm, O_END - O_GC), F32),
                        pltpu.VMEM((tm, CONV_ODD), F32),
                        pltpu.VMEM((tm, D_C), F32),
                        pltpu.VMEM((tm, D_C), F32),
                        pltpu.VMEM((1, D_C), F32),
                        pltpu.VMEM((H_D, DK_D, DV_D), F32),
                        pltpu.VMEM((tm, D_MODEL), BF16)],
        compiler_params=pltpu.CompilerParams(dimension_semantics=("arbitrary", "arbitrary"),
                                             vmem_limit_bytes=VMEM_LIMIT),
        name="odd_mixer",
    )(x, h0, st0, buf8, *weights)


def _ffn_kernel(x_ref, buf0_ref, wup_ref, wdw_ref, bdw_ref, wdown_ref, lng_ref, lnb_ref,
                y_ref, bufnew_ref, hbuf_s, *, tm):
    t = pl.program_id(1)
    nt = pl.num_programs(1)

    @pl.when(t == 0)
    def _():
        hbuf_s[0:8, :] = buf0_ref[0]

    x = x_ref[0]
    xb = x.astype(BF16)
    acc = jnp.zeros((tm, D_MODEL), F32)
    for jb in range(D_FF // FFN_COLS):
        cs = slice(jb * FFN_COLS, (jb + 1) * FFN_COLS)
        hg = _dot(xb, wup_ref[:, cs])
        hbuf_s[8:8 + tm, cs] = hg
        cv = (bdw_ref[:, cs] + wdw_ref[0:1, cs] * hbuf_s[6:6 + tm, cs] + wdw_ref[1:2, cs] * hbuf_s[7:7 + tm, cs]
              + wdw_ref[2:3, cs] * hg)
        hv = _dot(xb, wup_ref[:, D_FF + jb * FFN_COLS:D_FF + (jb + 1) * FFN_COLS])
        act = jax.nn.gelu(cv) * hv
        acc = acc + _dot(act.astype(BF16), wdown_ref[cs, :])
    hbuf_s[0:8, :] = hbuf_s[tm:tm + 8, :]
    y_ref[0] = _layernorm(ALPHA * x + acc, lng_ref[...], lnb_ref[...])

    @pl.when(t == nt - 1)
    def _():
        bufnew_ref[0] = hbuf_s[0:8, :]


def _ffn_layer(x, buf8, w, tm):
    bsz, L, _ = x.shape
    nt = L // tm
    full = lambda a: pl.BlockSpec(a.shape, lambda b, t: (0,) * a.ndim)
    weights = (w['wup'], w['wdw'], w['bdw'], w['wdown'], w['lng'], w['lnb'])
    return pl.pallas_call(
        functools.partial(_ffn_kernel, tm=tm),
        grid=(bsz, nt),
        in_specs=[pl.BlockSpec((1, tm, D_MODEL), lambda b, t: (b, t, 0)),
                  pl.BlockSpec((1, 8, D_FF), lambda b, t: (b, 0, 0))] + [full(a) for a in weights],
        out_specs=[pl.BlockSpec((1, tm, D_MODEL), lambda b, t: (b, t, 0)),
                   pl.BlockSpec((1, 8, D_FF), lambda b, t: (b, 0, 0))],
        out_shape=[jax.ShapeDtypeStruct((bsz, L, D_MODEL), F32),
                   jax.ShapeDtypeStruct((bsz, 8, D_FF), F32)],
        scratch_shapes=[pltpu.VMEM((tm + 8, D_FF), F32)],
        compiler_params=pltpu.CompilerParams(dimension_semantics=("arbitrary", "arbitrary"),
                                             vmem_limit_bytes=VMEM_LIMIT),
        name="conv_ffn",
    )(x, buf8, *weights)


def _row(v):
    return v.reshape(1, -1).astype(F32)


def _pad_rows(a, rows):
    return jnp.pad(a, ((0, rows - a.shape[0]), (0, 0)))


def _block_diag(w):
    h, d, _ = w.shape
    return jnp.einsum('hij,hg->higj', w, jnp.eye(h, dtype=w.dtype)).reshape(h * d, h * d)


def _prep_even(i, we_in, we_lr, be_lr, ge_gla, we_dw, be_dw, ge_cn, be_cn, we_out, ln1_g, ln1_b, l):
    call, masks = _gla_constants()
    w_in = we_in[i]
    lr0 = 2 * H_A * DK_A + 2 * H_A * DV_A
    win = jnp.concatenate([w_in[:, :lr0], w_in[:, lr0 + R_A:], w_in[:, lr0:lr0 + R_A],
                           jnp.zeros((D_MODEL, LANE - R_A), w_in.dtype)], axis=1).astype(BF16)
    return dict(win=win, wlr=_pad_rows(we_lr[i], LANE).astype(BF16), blr=_row(be_lr[i]), ggla=_row(ge_gla[i]),
                wdw=_pad_rows(we_dw[i], 32), bdw=_row(be_dw[i]), gcn=_row(ge_cn[i]), bcn=_row(be_cn[i]),
                wout=we_out[i].astype(BF16), lng=_row(ln1_g[l]), lnb=_row(ln1_b[l]),
                call=jnp.asarray(call, BF16), masks=jnp.asarray(masks, F32))


def _prep_odd(i, wo_in, wo_conv, bo_conv, wo_rg, bo_rg, wo_ig, bo_ig, lam_lru, a_log, dt_bias, go_delta, wo_out,
              ln1_g, ln1_b, l):
    w_in = wo_in[i]
    win = jnp.concatenate([w_in, jnp.zeros((D_MODEL, O_END - w_in.shape[1]), w_in.dtype)], axis=1).astype(BF16)
    head_row = lambda v: jnp.pad(v.astype(F32), (H_D, LANE - 2 * H_D)).reshape(1, LANE)
    r = np.arange(CHUNK)
    return dict(win=win, wcv=_pad_rows(wo_conv[i], 8), bcv=_row(bo_conv[i]),
                wg=jnp.concatenate([_block_diag(wo_rg[i]), _block_diag(wo_ig[i])], axis=1).astype(BF16),
                bg=_row(jnp.concatenate([bo_rg[i], bo_ig[i]])), lam=_row(lam_lru[i]),
                alog=head_row(a_log[i]), dtb=head_row(dt_bias[i]), gdl=_row(go_delta[i]),
                wout=wo_out[i].astype(BF16), lng=_row(ln1_g[l]), lnb=_row(ln1_b[l]),
                tri=jnp.asarray((r[None, :] <= r[:, None]).astype(np.float32), BF16))


def _prep_ffn(l, w_up, w_fdw, b_fdw, w_down, ln2_g, ln2_b):
    return dict(wup=w_up[l].astype(BF16), wdw=_pad_rows(w_fdw[l], 8), bdw=_row(b_fdw[l]),
                wdown=w_down[l].astype(BF16), lng=_row(ln2_g[l]), lnb=_row(ln2_b[l]))


def _front_pad(buf, rows):
    return jnp.pad(buf, ((0, 0), (rows - buf.shape[1], 0), (0, 0)))


def _trunk(x, states, mix_w, ffn_w):
    L = x.shape[1]
    tm = min(L, TIME_TILE)
    new_states = []
    for l in range(DEPTH):
        st = states[l]
        if l % 2 == 0:
            x, s_t, buf = _even_layer(x, jnp.swapaxes(st[0], 2, 3), _front_pad(st[1], 32), mix_w[l], tm)
            mix_new = (jnp.swapaxes(s_t, 2, 3), buf[:, 32 - (W_B - 1):])
        else:
            x, h, s, buf = _odd_layer(x, st[0][:, None, :], st[1], _front_pad(st[2], 8), mix_w[l], tm)
            mix_new = (h[:, 0], s, buf[:, 8 - (W_S - 1):])
        x, fbuf = _ffn_layer(x, _front_pad(st[-1], 8), ffn_w[l], tm)
        new_states.append((*mix_new, fbuf[:, 8 - (W_F - 1):]))
    return x, new_states


def _zero_states(bsz):
    z = lambda *s: jnp.zeros((bsz,) + s, F32)
    return [(z(H_A, DK_A, DV_A), z(W_B - 1, D_B), z(W_F - 1, D_FF)) if l % 2 == 0 else
            (z(D_C), z(H_D, DK_D, DV_D), z(W_S - 1, CONV_ODD), z(W_F - 1, D_FF)) for l in range(DEPTH)]


def kernel(x_prompt, x_sample, state_l0_gla, cache_l0_dwconv, cache_l0_ffn, state_l1_lru, state_l1_delta, cache_l1_conv, cache_l1_ffn, state_l2_gla, cache_l2_dwconv, cache_l2_ffn, state_l3_lru, state_l3_delta, cache_l3_conv, cache_l3_ffn, we_in, we_lr, be_lr, ge_gla, we_dw, be_dw, ge_cn, be_cn, we_out, wo_in, wo_conv, bo_conv, wo_rg, bo_rg, wo_ig, bo_ig, lam_lru, a_log, dt_bias, go_delta, wo_out, w_up, w_fdw, b_fdw, w_down, ln1_g, ln1_b, ln2_g, ln2_b):
    mix_w = []
    for l in range(DEPTH):
        if l % 2 == 0:
            mix_w.append(_prep_even(l // 2, we_in, we_lr, be_lr, ge_gla, we_dw, be_dw, ge_cn, be_cn, we_out,
                                    ln1_g, ln1_b, l))
        else:
            mix_w.append(_prep_odd(l // 2, wo_in, wo_conv, bo_conv, wo_rg, bo_rg, wo_ig, bo_ig, lam_lru, a_log,
                                   dt_bias, go_delta, wo_out, ln1_g, ln1_b, l))
    ffn_w = [_prep_ffn(l, w_up, w_fdw, b_fdw, w_down, ln2_g, ln2_b) for l in range(DEPTH)]
    y_prompt, new_p = _trunk(x_prompt, _zero_states(x_prompt.shape[0]), mix_w, ffn_w)
    sample_states = [(state_l0_gla, cache_l0_dwconv, cache_l0_ffn),
                     (state_l1_lru, state_l1_delta, cache_l1_conv, cache_l1_ffn),
                     (state_l2_gla, cache_l2_dwconv, cache_l2_ffn),
                     (state_l3_lru, state_l3_delta, cache_l3_conv, cache_l3_ffn)]
    y_sample, new_s = _trunk(x_sample, sample_states, mix_w, ffn_w)
    flat = lambda ns: [a for layer in ns for a in layer]
    return (y_prompt, y_sample, *flat(new_p), *flat(new_s))
```

```python
import functools

import numpy as np
import jax
import jax.numpy as jnp
from jax import lax
from jax.experimental import pallas as pl
from jax.experimental.pallas import tpu as pltpu

F32 = jnp.float32
BF16 = jnp.bfloat16

D_MODEL = 1024
DEPTH = 4
CHUNK = 64
H_A, DK_A, DV_A, R_A, TAU_A = 4, 64, 128, 16, 16.0
D_B, W_B = 512, 31
D_C, H_C, DH_C, LRU_C = 512, 8, 64, 8.0
H_D, DK_D, DV_D, W_S = 4, 128, 128, 4
D_FF, W_F = 2688, 3
ALPHA = (2 * DEPTH) ** 0.25
EPS = 1e-5
CONV_ODD = D_C + 2 * H_D * DK_D + H_D * DV_D

LANE = 128
TIME_TILE = 256
VMEM_LIMIT = 56 * 1024 * 1024
FFN_COLS = 896
GLA_LEVELS = 6

E_Q, E_K, E_V, E_G, E_GA, E_GB, E_LR, E_END = 0, 256, 512, 1024, 1536, 2048, 2560, 2688
O_CONV, O_GC, O_Z, O_BA, O_END = 0, 2048, 2560, 3072, 3200


def _dot(a, b):
    return jnp.dot(a, b, preferred_element_type=F32)


def _dot_nt(a, b):
    return lax.dot_general(a, b, (((1,), (1,)), ((), ())), preferred_element_type=F32)


def _dot_tn(a, b):
    return lax.dot_general(a, b, (((0,), (0,)), ((), ())), preferred_element_type=F32)


def _split_dot(m, x):
    hi = x.astype(BF16)
    lo = (x - hi.astype(F32)).astype(BF16)
    return _dot(m, hi) + _dot(m, lo)


def _sigmoid(x):
    return jax.nn.sigmoid(x)


def _silu(x):
    return x * jax.nn.sigmoid(x)


def _softplus(x):
    return jnp.maximum(x, 0.0) + jnp.log1p(jnp.exp(-jnp.abs(x)))


def _layernorm(x, g, b):
    mu = jnp.mean(x, axis=-1, keepdims=True)
    xc = x - mu
    var = jnp.mean(xc * xc, axis=-1, keepdims=True)
    return xc * lax.rsqrt(var + EPS) * g + b


def _rmsnorm(x):
    return x * lax.rsqrt(jnp.mean(x * x, axis=-1, keepdims=True) + EPS)


def _block_mask(rows, cols, rblk, cblk):
    r = np.arange(rows)[:, None] // rblk
    c = np.arange(cols)[None, :] // cblk
    return (r == c).astype(np.float32)


def _gla_constants():
    c = CHUNK
    r = np.arange(c)
    tri = (r[None, :] <= r[:, None]).astype(np.float32)
    rest = (r[None, :] > r[:, None]).astype(np.float32)
    mats = [tri, rest]
    masks = [np.eye(c, dtype=np.float32)]
    for lvl in range(GLA_LEVELS):
        m = (c // 2) >> lvl
        anchor = (r // (2 * m)) * (2 * m) + m - 1
        mats.append(tri - tri[anchor])
        same = (r[:, None] // (2 * m)) == (r[None, :] // (2 * m))
        masks.append((same & ((r[:, None] % (2 * m)) >= m) & ((r[None, :] % (2 * m)) < m)).astype(np.float32))
    return np.concatenate(mats, 0), np.tile(np.stack(masks, 0), (1, 1, H_A))


def _even_kernel(x_ref, st0_ref, buf0_ref, win_ref, wlr_ref, blr_ref, ggla_ref, wdw_ref, bdw_ref, gcn_ref, bcn_ref,
                 wout_ref, lng_ref, lnb_ref, call_ref, masks_ref, bdk_ref, bdv_ref, bdst_ref,
                 y_ref, stnew_ref, bufnew_ref,
                 proj_s, ubuf_s, state_s, cat_s, *, tm):
    t = pl.program_id(1)
    nt = pl.num_programs(1)

    @pl.when(t == 0)
    def _():
        state_s[...] = st0_ref[0]
        ubuf_s[0:32, :] = buf0_ref[0]

    x = x_ref[0]
    proj_s[...] = _dot(x.astype(BF16), win_ref[...])

    ubuf_s[32:32 + tm, :] = proj_s[:, E_GA:E_GB] * _sigmoid(proj_s[:, E_GB:E_LR])
    rb = 32
    for i in range(tm // rb):
        acc = jnp.broadcast_to(bdw_ref[...], (rb, D_B))
        for j in range(W_B):
            acc = acc + wdw_ref[j:j + 1, :] * ubuf_s[i * rb + 2 + j:i * rb + 2 + j + rb, :]
        ob = _silu(_layernorm(acc, gcn_ref[...], bcn_ref[...]))
        cat_s[i * rb:(i + 1) * rb, D_B:2 * D_B] = ob.astype(BF16)
    ubuf_s[0:32, :] = ubuf_s[tm:tm + 32, :]

    def chunk(c, carry):
        r0 = pl.multiple_of(c * CHUNK, CHUNK)
        rows = pl.ds(r0, CHUNK)
        q = proj_s[rows, E_Q:E_K] * (DK_A ** -0.5)
        k = proj_s[rows, E_K:E_V]
        vb = proj_s[rows, E_V:E_G].astype(BF16)
        z = _dot(proj_s[rows, E_LR:E_END].astype(BF16), wlr_ref[...]) + blr_ref[...]
        la = -_softplus(-z) * (1.0 / TAU_A)
        e_all = _split_dot(call_ref[...], la)
        bcum = e_all[0:CHUNK]
        qe = (q * jnp.exp(bcum)).astype(BF16)
        kr = (k * jnp.exp(e_all[CHUNK:2 * CHUNK])).astype(BF16)
        dlast = jnp.exp(bcum[CHUNK - 1:CHUNK, :])
        p = jnp.zeros((CHUNK, H_A * CHUNK), F32)
        for lvl in range(GLA_LEVELS + 1):
            if lvl == 0:
                qf, kf = q, k
            else:
                f = jnp.exp(-jnp.abs(e_all[(1 + lvl) * CHUNK:(2 + lvl) * CHUNK]))
                qf, kf = q * f, k * f
            kbd = jnp.tile(kf.astype(BF16), (H_A, 1)) * bdk_ref[...]
            p = p + masks_ref[lvl] * _dot_nt(qf.astype(BF16), kbd)
        vbd = jnp.tile(vb, (H_A, 1)) * bdv_ref[...]
        st = state_s[...]
        o = _dot_nt(qe, st.astype(BF16)) + _dot(p.astype(BF16), vbd)
        state_s[...] = st * dlast + bdst_ref[...] * _dot_tn(vb, kr)
        for h in range(H_A):
            hs = slice(h * DV_A, (h + 1) * DV_A)
            gate = proj_s[rows, E_G + h * DV_A:E_G + (h + 1) * DV_A]
            oa = _rmsnorm(o[:, hs]) * ggla_ref[:, hs] * _silu(gate)
            cat_s[rows, hs] = oa.astype(BF16)
        return carry

    lax.fori_loop(0, tm // CHUNK, chunk, 0)

    y = _dot(cat_s[...], wout_ref[...])
    y_ref[0] = _layernorm(ALPHA * x + y, lng_ref[...], lnb_ref[...])

    @pl.when(t == nt - 1)
    def _():
        stnew_ref[0] = state_s[...]
        bufnew_ref[0] = ubuf_s[0:32, :]


def _even_layer(x, st_t, buf32, w, tm):
    bsz, L, _ = x.shape
    nt = L // tm
    full = lambda a: pl.BlockSpec(a.shape, lambda b, t: (0,) * a.ndim)
    weights = (w['win'], w['wlr'], w['blr'], w['ggla'], w['wdw'], w['bdw'], w['gcn'], w['bcn'], w['wout'],
               w['lng'], w['lnb'], w['call'], w['masks'], w['bdk'], w['bdv'], w['bdst'])
    return pl.pallas_call(
        functools.partial(_even_kernel, tm=tm),
        grid=(bsz, nt),
        in_specs=[pl.BlockSpec((1, tm, D_MODEL), lambda b, t: (b, t, 0)),
                  pl.BlockSpec((1, H_A * DV_A, H_A * DK_A), lambda b, t: (b, 0, 0)),
                  pl.BlockSpec((1, 32, D_B), lambda b, t: (b, 0, 0))] + [full(a) for a in weights],
        out_specs=[pl.BlockSpec((1, tm, D_MODEL), lambda b, t: (b, t, 0)),
                   pl.BlockSpec((1, H_A * DV_A, H_A * DK_A), lambda b, t: (b, 0, 0)),
                   pl.BlockSpec((1, 32, D_B), lambda b, t: (b, 0, 0))],
        out_shape=[jax.ShapeDtypeStruct((bsz, L, D_MODEL), F32),
                   jax.ShapeDtypeStruct((bsz, H_A * DV_A, H_A * DK_A), F32),
                   jax.ShapeDtypeStruct((bsz, 32, D_B), F32)],
        scratch_shapes=[pltpu.VMEM((tm, E_END), F32),
                        pltpu.VMEM((tm + 32, D_B), F32),
                        pltpu.VMEM((H_A * DV_A, H_A * DK_A), F32),
                        pltpu.VMEM((tm, D_MODEL), BF16)],
        compiler_params=pltpu.CompilerParams(dimension_semantics=("arbitrary", "arbitrary"),
                                             vmem_limit_bytes=VMEM_LIMIT),
        name="even_mixer",
    )(x, st_t, buf32, *weights)


def _odd_kernel(x_ref, h0_ref, st0_ref, buf0_ref, win_ref, wcv_ref, bcv_ref, wg_ref, bg_ref, lam_ref, alog_ref,
                dtb_ref, gdl_ref, wout_ref, lng_ref, lnb_ref, tri_ref, bd_ref, bdk_ref,
                y_ref, hnew_ref, stnew_ref, bufnew_ref,
                cbuf_s, rest_s, cv_s, a_s, bx_s, hrow_s, state_s, cat_s, *, tm):
    t = pl.program_id(1)
    nt = pl.num_programs(1)

    @pl.when(t == 0)
    def _():
        hrow_s[...] = h0_ref[0]
        state_s[...] = st0_ref[0]
        cbuf_s[0:8, :] = buf0_ref[0]

    x = x_ref[0]
    xb = x.astype(BF16)
    cbuf_s[8:8 + tm, :] = _dot(xb, win_ref[:, O_CONV:O_GC])
    rest_s[...] = _dot(xb, win_ref[:, O_GC:O_END])

    rb = 32
    for i in range(tm // rb):
        for cb in range(CONV_ODD // D_C):
            cs = slice(cb * D_C, (cb + 1) * D_C)
            acc = jnp.broadcast_to(bcv_ref[:, cs], (rb, D_C))
            for j in range(W_S):
                acc = acc + wcv_ref[j:j + 1, cs] * cbuf_s[i * rb + 5 + j:i * rb + 5 + j + rb, cs]
            cv_s[i * rb:(i + 1) * rb, cs] = acc if cb == 0 else _silu(acc)
    cbuf_s[0:8, :] = cbuf_s[tm:tm + 8, :]

    xc = cv_s[:, 0:D_C]
    gates = _dot(xc.astype(BF16), wg_ref[...]) + bg_ref[...]
    log_a = LRU_C * _sigmoid(gates[:, 0:D_C]) * (-_softplus(-lam_ref[...]))
    a_s[...] = jnp.exp(log_a)
    one_m_a2 = -jnp.tanh(log_a) * (jnp.exp(2.0 * log_a) + 1.0)
    bx_s[...] = jnp.sqrt(one_m_a2) * (_sigmoid(gates[:, D_C:2 * D_C]) * xc)

    def lru_row(i, h):
        h = a_s[pl.ds(i, 1), :] * h + bx_s[pl.ds(i, 1), :]
        bx_s[pl.ds(i, 1), :] = h
        return h

    hrow_s[...] = lax.fori_loop(0, tm, lru_row, hrow_s[...])
    cat_s[:, 0:D_C] = (bx_s[...] * jax.nn.gelu(rest_s[:, 0:D_C])).astype(BF16)

    row = lax.broadcasted_iota(jnp.int32, (CHUNK, H_D * CHUNK), 0)
    col = lax.broadcasted_iota(jnp.int32, (CHUNK, H_D * CHUNK), 1) % CHUNK
    causal = (row >= col).astype(F32)
    strict = (row > col).astype(F32)
    eye = (row == col).astype(F32)
    qoff, koff, voff = D_C, D_C + H_D * DK_D, D_C + 2 * H_D * DK_D

    def blockdiag(m):
        return jnp.tile(m.astype(BF16), (H_D, 1)) * bd_ref[...]

    pre = []
    for c in range(tm // CHUNK):
        rows = slice(c * CHUNK, (c + 1) * CHUNK)
        ba = rest_s[rows, O_BA - O_GC:O_END - O_GC]
        beta_all = _sigmoid(ba)
        g_all = -jnp.exp(alog_ref[...]) * _softplus(ba + dtb_ref[...])
        gcum_all = _split_dot(tri_ref[...], g_all)
        gb = jnp.concatenate([jnp.broadcast_to(g_all[:, H_D + h:H_D + h + 1], (CHUNK, CHUNK)) for h in range(H_D)],
                             axis=1)
        dmat = _split_dot(tri_ref[...], gb * strict)
        edm = jnp.exp(dmat)
        kn_l, kb_l, qn_l, rhs_l, qg_l, kg_l, eg_l = [], [], [], [], [], [], []
        for h in range(H_D):
            qh = cv_s[rows, qoff + h * DK_D:qoff + (h + 1) * DK_D]
            kh = cv_s[rows, koff + h * DK_D:koff + (h + 1) * DK_D]
            vh = cv_s[rows, voff + h * DV_D:voff + (h + 1) * DV_D]
            qn = qh * lax.rsqrt(jnp.sum(qh * qh, axis=-1, keepdims=True) + 1e-6) * (DK_D ** -0.5)
            kn = kh * lax.rsqrt(jnp.sum(kh * kh, axis=-1, keepdims=True) + 1e-6)
            beta = beta_all[:, h:h + 1]
            gc = gcum_all[:, H_D + h:H_D + h + 1]
            glast = gc[CHUNK - 1:CHUNK, :]
            egc = jnp.exp(gc)
            kb = kn * beta
            kn_l.append(kn.astype(BF16))
            kb_l.append(kb.astype(BF16))
            qn_l.append(qn.astype(BF16))
            rhs_l.append(jnp.concatenate([vh * beta, kb * egc], axis=-1).astype(BF16))
            qg_l.append((qn * egc).astype(BF16))
            kg_l.append((kn * jnp.exp(glast - gc)).astype(BF16))
            eg_l.append(jnp.exp(glast))
        lhs = jnp.concatenate([jnp.concatenate(kb_l, axis=1), jnp.concatenate(qn_l, axis=1)], axis=0)
        kbd = jnp.tile(jnp.concatenate(kn_l, axis=1), (H_D, 1)) * bdk_ref[...]
        aq = _dot_nt(lhs, kbd)
        n = -(aq[0:CHUNK] * edm * strict)
        attn = (aq[CHUNK:2 * CHUNK] * edm * causal).astype(BF16)
        tt = eye + n
        pw = n
        for _ in range(5):
            pwb = pw.astype(BF16)
            pw = _dot(pwb, blockdiag(pwb))
            tt = tt + _dot(pw.astype(BF16), blockdiag(tt))
        ttb = tt.astype(BF16)
        sol_l = [_dot(ttb[:, h * CHUNK:(h + 1) * CHUNK], rhs_l[h]) for h in range(H_D)]
        pre.append((sol_l, attn, qg_l, kg_l, eg_l))

    for c in range(tm // CHUNK):
        rows = slice(c * CHUNK, (c + 1) * CHUNK)
        sol_l, attn, qg_l, kg_l, eg_l = pre[c]
        for h in range(H_D):
            st = state_s[h]
            ws = _dot(jnp.concatenate([sol_l[h][:, DV_D:].astype(BF16), qg_l[h]], axis=0), st.astype(BF16))
            vnb = (sol_l[h][:, 0:DV_D] - ws[0:CHUNK]).astype(BF16)
            o = ws[CHUNK:2 * CHUNK] + _dot(attn[:, h * CHUNK:(h + 1) * CHUNK], vnb)
            state_s[h] = eg_l[h] * st + _dot_tn(kg_l[h], vnb)
            zg = rest_s[rows, O_Z - O_GC + h * DV_D:O_Z - O_GC + (h + 1) * DV_D]
            od = _rmsnorm(o) * gdl_ref[:, h * DV_D:(h + 1) * DV_D] * _silu(zg)
            cat_s[rows, D_C + h * DV_D:D_C + (h + 1) * DV_D] = od.astype(BF16)

    y = _dot(cat_s[...], wout_ref[...])
    y_ref[0] = _layernorm(ALPHA * x + y, lng_ref[...], lnb_ref[...])

    @pl.when(t == nt - 1)
    def _():
        hnew_ref[0] = hrow_s[...]
        stnew_ref[0] = state_s[...]
        bufnew_ref[0] = cbuf_s[0:8, :]


def _odd_layer(x, h0, st0, buf8, w, tm):
    bsz, L, _ = x.shape
    nt = L // tm
    full = lambda a: pl.BlockSpec(a.shape, lambda b, t: (0,) * a.ndim)
    weights = (w['win'], w['wcv'], w['bcv'], w['wg'], w['bg'], w['lam'], w['alog'], w['dtb'], w['gdl'], w['wout'],
               w['lng'], w['lnb'], w['tri'], w['bd'], w['bdk'])
    return pl.pallas_call(
        functools.partial(_odd_kernel, tm=tm),
        grid=(bsz, nt),
        in_specs=[pl.BlockSpec((1, tm, D_MODEL), lambda b, t: (b, t, 0)),
                  pl.BlockSpec((1, 1, D_C), lambda b, t: (b, 0, 0)),
                  pl.BlockSpec((1, H_D, DK_D, DV_D), lambda b, t: (b, 0, 0, 0)),
                  pl.BlockSpec((1, 8, CONV_ODD), lambda b, t: (b, 0, 0))] + [full(a) for a in weights],
        out_specs=[pl.BlockSpec((1, tm, D_MODEL), lambda b, t: (b, t, 0)),
                   pl.BlockSpec((1, 1, D_C), lambda b, t: (b, 0, 0)),
                   pl.BlockSpec((1, H_D, DK_D, DV_D), lambda b, t: (b, 0, 0, 0)),
                   pl.BlockSpec((1, 8, CONV_ODD), lambda b, t: (b, 0, 0))],
        out_shape=[jax.ShapeDtypeStruct((bsz, L, D_MODEL), F32),
                   jax.ShapeDtypeStruct((bsz, 1, D_C), F32),
                   jax.ShapeDtypeStruct((bsz, H_D, DK_D, DV_D), F32),
                   jax.ShapeDtypeStruct((bsz, 8, CONV_ODD), F32)],
        scratch_shapes=[pltpu.VMEM((tm + 8, CONV_ODD), F32),
                        pltpu.VMEM((tm, O_END - O_GC), F32),
                        pltpu.VMEM((tm, CONV_ODD), F32),
                        pltpu.VMEM((tm, D_C), F32),
                        pltpu.VMEM((tm, D_C), F32),
                        pltpu.VMEM((1, D_C), F32),
                        pltpu.VMEM((H_D, DK_D, DV_D), F32),
                        pltpu.VMEM((tm, D_MODEL), BF16)],
        compiler_params=pltpu.CompilerParams(dimension_semantics=("arbitrary", "arbitrary"),
                                             vmem_limit_bytes=VMEM_LIMIT),
        name="odd_mixer",
    )(x, h0, st0, buf8, *weights)


def _ffn_kernel(x_ref, buf0_ref, wup_ref, wdw_ref, bdw_ref, wdown_ref, lng_ref, lnb_ref,
                y_ref, bufnew_ref, hbuf_s, *, tm):
    t = pl.program_id(1)
    nt = pl.num_programs(1)

    @pl.when(t == 0)
    def _():
        hbuf_s[0:8, :] = buf0_ref[0]

    x = x_ref[0]
    xb = x.astype(BF16)
    acc = jnp.zeros((tm, D_MODEL), F32)
    for jb in range(D_FF // FFN_COLS):
        cs = slice(jb * FFN_COLS, (jb + 1) * FFN_COLS)
        hg = _dot(xb, wup_ref[:, cs])
        hbuf_s[8:8 + tm, cs] = hg
        cv = (bdw_ref[:, cs] + wdw_ref[0:1, cs] * hbuf_s[6:6 + tm, cs] + wdw_ref[1:2, cs] * hbuf_s[7:7 + tm, cs]
              + wdw_ref[2:3, cs] * hg)
        hv = _dot(xb, wup_ref[:, D_FF + jb * FFN_COLS:D_FF + (jb + 1) * FFN_COLS])
        act = jax.nn.gelu(cv) * hv
        acc = acc + _dot(act.astype(BF16), wdown_ref[cs, :])
    hbuf_s[0:8, :] = hbuf_s[tm:tm + 8, :]
    y_ref[0] = _layernorm(ALPHA * x + acc, lng_ref[...], lnb_ref[...])

    @pl.when(t == nt - 1)
    def _():
        bufnew_ref[0] = hbuf_s[0:8, :]


def _ffn_layer(x, buf8, w, tm):
    bsz, L, _ = x.shape
    nt = L // tm
    full = lambda a: pl.BlockSpec(a.shape, lambda b, t: (0,) * a.ndim)
    weights = (w['wup'], w['wdw'], w['bdw'], w['wdown'], w['lng'], w['lnb'])
    return pl.pallas_call(
        functools.partial(_ffn_kernel, tm=tm),
        grid=(bsz, nt),
        in_specs=[pl.BlockSpec((1, tm, D_MODEL), lambda b, t: (b, t, 0)),
                  pl.BlockSpec((1, 8, D_FF), lambda b, t: (b, 0, 0))] + [full(a) for a in weights],
        out_specs=[pl.BlockSpec((1, tm, D_MODEL), lambda b, t: (b, t, 0)),
                   pl.BlockSpec((1, 8, D_FF), lambda b, t: (b, 0, 0))],
        out_shape=[jax.ShapeDtypeStruct((bsz, L, D_MODEL), F32),
                   jax.ShapeDtypeStruct((bsz, 8, D_FF), F32)],
        scratch_shapes=[pltpu.VMEM((tm + 8, D_FF), F32)],
        compiler_params=pltpu.CompilerParams(dimension_semantics=("arbitrary", "arbitrary"),
                                             vmem_limit_bytes=VMEM_LIMIT),
        name="conv_ffn",
    )(x, buf8, *weights)


def _row(v):
    return v.reshape(1, -1).astype(F32)


def _pad_rows(a, rows):
    return jnp.pad(a, ((0, rows - a.shape[0]), (0, 0)))


def _block_diag(w):
    h, d, _ = w.shape
    return jnp.einsum('hij,hg->higj', w, jnp.eye(h, dtype=w.dtype)).reshape(h * d, h * d)


def _prep_even(i, we_in, we_lr, be_lr, ge_gla, we_dw, be_dw, ge_cn, be_cn, we_out, ln1_g, ln1_b, l):
    call, masks = _gla_constants()
    w_in = we_in[i]
    lr0 = 2 * H_A * DK_A + 2 * H_A * DV_A
    win = jnp.concatenate([w_in[:, :lr0], w_in[:, lr0 + R_A:], w_in[:, lr0:lr0 + R_A],
                           jnp.zeros((D_MODEL, LANE - R_A), w_in.dtype)], axis=1).astype(BF16)
    return dict(win=win, wlr=_pad_rows(we_lr[i], LANE).astype(BF16), blr=_row(be_lr[i]), ggla=_row(ge_gla[i]),
                wdw=_pad_rows(we_dw[i], 32), bdw=_row(be_dw[i]), gcn=_row(ge_cn[i]), bcn=_row(be_cn[i]),
                wout=we_out[i].astype(BF16), lng=_row(ln1_g[l]), lnb=_row(ln1_b[l]),
                call=jnp.asarray(call, BF16), masks=jnp.asarray(masks, F32),
                bdk=jnp.asarray(_block_mask(H_A * CHUNK, H_A * DK_A, CHUNK, DK_A), BF16),
                bdv=jnp.asarray(_block_mask(H_A * CHUNK, H_A * DV_A, CHUNK, DV_A), BF16),
                bdst=jnp.asarray(_block_mask(H_A * DV_A, H_A * DK_A, DV_A, DK_A), F32))


def _prep_odd(i, wo_in, wo_conv, bo_conv, wo_rg, bo_rg, wo_ig, bo_ig, lam_lru, a_log, dt_bias, go_delta, wo_out,
              ln1_g, ln1_b, l):
    w_in = wo_in[i]
    win = jnp.concatenate([w_in, jnp.zeros((D_MODEL, O_END - w_in.shape[1]), w_in.dtype)], axis=1).astype(BF16)
    head_row = lambda v: jnp.pad(v.astype(F32), (H_D, LANE - 2 * H_D)).reshape(1, LANE)
    r = np.arange(CHUNK)
    return dict(win=win, wcv=_pad_rows(wo_conv[i], 8), bcv=_row(bo_conv[i]),
                wg=jnp.concatenate([_block_diag(wo_rg[i]), _block_diag(wo_ig[i])], axis=1).astype(BF16),
                bg=_row(jnp.concatenate([bo_rg[i], bo_ig[i]])), lam=_row(lam_lru[i]),
                alog=head_row(a_log[i]), dtb=head_row(dt_bias[i]), gdl=_row(go_delta[i]),
                wout=wo_out[i].astype(BF16), lng=_row(ln1_g[l]), lnb=_row(ln1_b[l]),
                tri=jnp.asarray((r[None, :] <= r[:, None]).astype(np.float32), BF16),
                bd=jnp.asarray(_block_mask(H_D * CHUNK, H_D * CHUNK, CHUNK, CHUNK), BF16),
                bdk=jnp.asarray(_block_mask(H_D * CHUNK, H_D * DK_D, CHUNK, DK_D), BF16))


def _prep_ffn(l, w_up, w_fdw, b_fdw, w_down, ln2_g, ln2_b):
    return dict(wup=w_up[l].astype(BF16), wdw=_pad_rows(w_fdw[l], 8), bdw=_row(b_fdw[l]),
                wdown=w_down[l].astype(BF16), lng=_row(ln2_g[l]), lnb=_row(ln2_b[l]))


def _front_pad(buf, rows):
    return jnp.pad(buf, ((0, 0), (rows - buf.shape[1], 0), (0, 0)))


def _gla_state_to_blockdiag(s):
    s_t = jnp.swapaxes(s, 2, 3)
    return jnp.concatenate([jnp.pad(s_t[:, h], ((0, 0), (0, 0), (h * DK_A, (H_A - 1 - h) * DK_A)))
                            for h in range(H_A)], axis=1)


def _gla_state_from_blockdiag(s_bd):
    bsz = s_bd.shape[0]
    s5 = s_bd.reshape(bsz, H_A, DV_A, H_A, DK_A)
    return jnp.stack([jnp.swapaxes(s5[:, h, :, h, :], 1, 2) for h in range(H_A)], axis=1)


def _trunk(x, states, mix_w, ffn_w):
    L = x.shape[1]
    tm = min(L, TIME_TILE)
    new_states = []
    for l in range(DEPTH):
        st = states[l]
        if l % 2 == 0:
            x, s_bd, buf = _even_layer(x, _gla_state_to_blockdiag(st[0]), _front_pad(st[1], 32), mix_w[l], tm)
            mix_new = (_gla_state_from_blockdiag(s_bd), buf[:, 32 - (W_B - 1):])
        else:
            x, h, s, buf = _odd_layer(x, st[0][:, None, :], st[1], _front_pad(st[2], 8), mix_w[l], tm)
            mix_new = (h[:, 0], s, buf[:, 8 - (W_S - 1):])
        x, fbuf = _ffn_layer(x, _front_pad(st[-1], 8), ffn_w[l], tm)
        new_states.append((*mix_new, fbuf[:, 8 - (W_F - 1):]))
    return x, new_states


def _zero_states(bsz):
    z = lambda *s: jnp.zeros((bsz,) + s, F32)
    return [(z(H_A, DK_A, DV_A), z(W_B - 1, D_B), z(W_F - 1, D_FF)) if l % 2 == 0 else
            (z(D_C), z(H_D, DK_D, DV_D), z(W_S - 1, CONV_ODD), z(W_F - 1, D_FF)) for l in range(DEPTH)]


def kernel(x_prompt, x_sample, state_l0_gla, cache_l0_dwconv, cache_l0_ffn, state_l1_lru, state_l1_delta, cache_l1_conv, cache_l1_ffn, state_l2_gla, cache_l2_dwconv, cache_l2_ffn, state_l3_lru, state_l3_delta, cache_l3_conv, cache_l3_ffn, we_in, we_lr, be_lr, ge_gla, we_dw, be_dw, ge_cn, be_cn, we_out, wo_in, wo_conv, bo_conv, wo_rg, bo_rg, wo_ig, bo_ig, lam_lru, a_log, dt_bias, go_delta, wo_out, w_up, w_fdw, b_fdw, w_down, ln1_g, ln1_b, ln2_g, ln2_b):
    mix_w = []
    for l in range(DEPTH):
        if l % 2 == 0:
            mix_w.append(_prep_even(l // 2, we_in, we_lr, be_lr, ge_gla, we_dw, be_dw, ge_cn, be_cn, we_out,
                                    ln1_g, ln1_b, l))
        else:
            mix_w.append(_prep_odd(l // 2, wo_in, wo_conv, bo_conv, wo_rg, bo_rg, wo_ig, bo_ig, lam_lru, a_log,
                                   dt_bias, go_delta, wo_out, ln1_g, ln1_b, l))
    ffn_w = [_prep_ffn(l, w_up, w_fdw, b_fdw, w_down, ln2_g, ln2_b) for l in range(DEPTH)]
    y_prompt, new_p = _trunk(x_prompt, _zero_states(x_prompt.shape[0]), mix_w, ffn_w)
    sample_states = [(state_l0_gla, cache_l0_dwconv, cache_l0_ffn),
                     (state_l1_lru, state_l1_delta, cache_l1_conv, cache_l1_ffn),
                     (state_l2_gla, cache_l2_dwconv, cache_l2_ffn),
                     (state_l3_lru, state_l3_delta, cache_l3_conv, cache_l3_ffn)]
    y_sample, new_s = _trunk(x_sample, sample_states, mix_w, ffn_w)
    flat = lambda ns: [a for layer in ns for a in layer]
    return (y_prompt, y_sample, *flat(new_p), *flat(new_s))
```

```python
import functools

import numpy as np
import jax
import jax.numpy as jnp
from jax import lax
from jax.experimental import pallas as pl
from jax.experimental.pallas import tpu as pltpu

F32 = jnp.float32
BF16 = jnp.bfloat16

D_MODEL = 1024
DEPTH = 4
CHUNK = 64
H_A, DK_A, DV_A, R_A, TAU_A = 4, 64, 128, 16, 16.0
D_B, W_B = 512, 31
D_C, H_C, DH_C, LRU_C = 512, 8, 64, 8.0
H_D, DK_D, DV_D, W_S = 4, 128, 128, 4
D_FF, W_F = 2688, 3
ALPHA = (2 * DEPTH) ** 0.25
EPS = 1e-5
CONV_ODD = D_C + 2 * H_D * DK_D + H_D * DV_D

LANE = 128
TIME_TILE = 256
VMEM_LIMIT = 56 * 1024 * 1024
MXU_TILE = 256
FFN_COLS = 5 * MXU_TILE
GLA_LEVELS = 6

E_Q, E_K, E_V, E_G, E_GA, E_GB, E_LR, E_END = 0, 256, 512, 1024, 1536, 2048, 2560, 2688
O_CONV, O_GC, O_Z, O_BA, O_END = 0, 2048, 2560, 3072, 3200


def _dot(a, b):
    return jnp.dot(a, b, preferred_element_type=F32)


def _dot_nt(a, b):
    return lax.dot_general(a, b, (((1,), (1,)), ((), ())), preferred_element_type=F32)


def _dot_tn(a, b):
    return lax.dot_general(a, b, (((0,), (0,)), ((), ())), preferred_element_type=F32)


def _split_dot(m, x):
    hi = x.astype(BF16)
    lo = (x - hi.astype(F32)).astype(BF16)
    return _dot(m, hi) + _dot(m, lo)


def _sigmoid(x):
    return jax.nn.sigmoid(x)


def _silu(x):
    return x * jax.nn.sigmoid(x)


def _softplus(x):
    return jnp.maximum(x, 0.0) + jnp.log1p(jnp.exp(-jnp.abs(x)))


def _layernorm(x, g, b):
    mu = jnp.mean(x, axis=-1, keepdims=True)
    xc = x - mu
    var = jnp.mean(xc * xc, axis=-1, keepdims=True)
    return xc * lax.rsqrt(var + EPS) * g + b


def _rmsnorm(x):
    return x * lax.rsqrt(jnp.mean(x * x, axis=-1, keepdims=True) + EPS)


def _block_mask(rows, cols, rblk, cblk):
    r = np.arange(rows)[:, None] // rblk
    c = np.arange(cols)[None, :] // cblk
    return (r == c).astype(np.float32)


def _gla_constants(tm):
    c = CHUNK
    r = np.arange(c)
    tri = (r[None, :] <= r[:, None]).astype(np.float32)
    rest = (r[None, :] > r[:, None]).astype(np.float32)
    mats = [tri, rest]
    masks = [np.eye(c, dtype=np.float32)]
    for lvl in range(GLA_LEVELS):
        m = (c // 2) >> lvl
        anchor = (r // (2 * m)) * (2 * m) + m - 1
        mats.append(tri - tri[anchor])
        same = (r[:, None] // (2 * m)) == (r[None, :] // (2 * m))
        masks.append((same & ((r[:, None] % (2 * m)) >= m) & ((r[None, :] % (2 * m)) < m)).astype(np.float32))
    nc = tm // c
    call = np.concatenate([np.kron(np.eye(nc, dtype=np.float32), m) for m in mats], 0)
    return jnp.asarray(call, BF16), jnp.asarray(np.tile(np.stack(masks, 0), (1, nc, H_A)), F32)


def _even_kernel(x_ref, st0_ref, buf0_ref, win_ref, wlr_ref, blr_ref, ggla_ref, wdw_ref, bdw_ref, gcn_ref, bcn_ref,
                 wout_ref, lng_ref, lnb_ref, call_ref, masks_ref, bdk_ref, bdv_ref, bdst_ref,
                 y_ref, stnew_ref, bufnew_ref,
                 proj_s, shift_s, state_s, cat_s, *, tm):
    t = pl.program_id(1)
    nt = pl.num_programs(1)

    @pl.when(t == 0)
    def _():
        state_s[...] = st0_ref[0]
        shift_s[0, 0:32, :] = buf0_ref[0]

    x = x_ref[0]
    proj_s[...] = _dot(x.astype(BF16), win_ref[...])

    shift_s[0, 32:32 + tm, :] = proj_s[:, E_GA:E_GB] * _sigmoid(proj_s[:, E_GB:E_LR])
    for s in range(1, 8):
        shift_s[s, 0:tm + 24, :] = shift_s[0, s:s + tm + 24, :]
    rb = 32
    for i in range(tm // rb):
        acc = jnp.broadcast_to(bdw_ref[...], (rb, D_B))
        for j in range(W_B):
            off = i * rb + 2 + j
            acc = acc + wdw_ref[j:j + 1, :] * shift_s[off % 8, off - off % 8:off - off % 8 + rb, :]
        ob = _silu(_layernorm(acc, gcn_ref[...], bcn_ref[...]))
        cat_s[i * rb:(i + 1) * rb, D_B:2 * D_B] = ob.astype(BF16)
    shift_s[0, 0:32, :] = shift_s[0, tm:tm + 32, :]

    nc = tm // CHUNK
    chunks = [slice(c * CHUNK, (c + 1) * CHUNK) for c in range(nc)]

    def per_chunk(fn):
        return jnp.concatenate([fn(cs) for cs in chunks], axis=0)

    q = proj_s[:, E_Q:E_K] * (DK_A ** -0.5)
    k = proj_s[:, E_K:E_V]
    vb = proj_s[:, E_V:E_G].astype(BF16)
    z = _dot(proj_s[:, E_LR:E_END].astype(BF16), wlr_ref[...]) + blr_ref[...]
    la = -_softplus(-z) * (1.0 / TAU_A)
    e_all = _split_dot(call_ref[...], la)
    bcum = e_all[0:tm]
    brest = e_all[tm:2 * tm]
    qe = (q * jnp.exp(bcum)).astype(BF16)
    kr = (k * jnp.exp(brest)).astype(BF16)
    dlast = jnp.exp(bcum + brest)
    kv = [bdst_ref[...] * _dot_tn(vb[cs], kr[cs]) for cs in chunks]
    p = jnp.zeros((tm, H_A * CHUNK), F32)
    for lvl in range(GLA_LEVELS + 1):
        if lvl == 0:
            qf, kf = q.astype(BF16), k.astype(BF16)
        else:
            f = jnp.exp(-jnp.abs(e_all[(1 + lvl) * tm:(2 + lvl) * tm]))
            qf, kf = (q * f).astype(BF16), (k * f).astype(BF16)
        p = p + masks_ref[lvl] * per_chunk(lambda cs: _dot_nt(qf[cs], jnp.tile(kf[cs], (H_A, 1)) * bdk_ref[...]))
    pb = p.astype(BF16)
    o_intra = [_dot(pb[cs], jnp.tile(vb[cs], (H_A, 1)) * bdv_ref[...]) for cs in chunks]
    st = state_s[...]
    sts = []
    for c in range(nc):
        sts.append(st.astype(BF16))
        st = st * dlast[c * CHUNK:c * CHUNK + 1, :] + kv[c]
    state_s[...] = st
    o = jnp.concatenate([_dot_nt(qe[cs], sts[c]) + o_intra[c] for c, cs in enumerate(chunks)], axis=0)
    for h in range(H_A):
        hs = slice(h * DV_A, (h + 1) * DV_A)
        oa = _rmsnorm(o[:, hs]) * ggla_ref[:, hs] * _silu(proj_s[:, E_G + h * DV_A:E_G + (h + 1) * DV_A])
        cat_s[:, hs] = oa.astype(BF16)

    y = _dot(cat_s[...], wout_ref[...])
    y_ref[0] = _layernorm(ALPHA * x + y, lng_ref[...], lnb_ref[...])

    @pl.when(t == nt - 1)
    def _():
        stnew_ref[0] = state_s[...]
        bufnew_ref[0] = shift_s[0, 0:32, :]


def _even_layer(x, st_t, buf32, w, tm):
    bsz, L, _ = x.shape
    nt = L // tm
    full = lambda a: pl.BlockSpec(a.shape, lambda b, t: (0,) * a.ndim)
    weights = (w['win'], w['wlr'], w['blr'], w['ggla'], w['wdw'], w['bdw'], w['gcn'], w['bcn'], w['wout'],
               w['lng'], w['lnb'], *_gla_constants(tm), w['bdk'], w['bdv'], w['bdst'])
    return pl.pallas_call(
        functools.partial(_even_kernel, tm=tm),
        grid=(bsz, nt),
        in_specs=[pl.BlockSpec((1, tm, D_MODEL), lambda b, t: (b, t, 0)),
                  pl.BlockSpec((1, H_A * DV_A, H_A * DK_A), lambda b, t: (b, 0, 0)),
                  pl.BlockSpec((1, 32, D_B), lambda b, t: (b, 0, 0))] + [full(a) for a in weights],
        out_specs=[pl.BlockSpec((1, tm, D_MODEL), lambda b, t: (b, t, 0)),
                   pl.BlockSpec((1, H_A * DV_A, H_A * DK_A), lambda b, t: (b, 0, 0)),
                   pl.BlockSpec((1, 32, D_B), lambda b, t: (b, 0, 0))],
        out_shape=[jax.ShapeDtypeStruct((bsz, L, D_MODEL), F32),
                   jax.ShapeDtypeStruct((bsz, H_A * DV_A, H_A * DK_A), F32),
                   jax.ShapeDtypeStruct((bsz, 32, D_B), F32)],
        scratch_shapes=[pltpu.VMEM((tm, E_END), F32),
                        pltpu.VMEM((8, tm + 32, D_B), F32),
                        pltpu.VMEM((H_A * DV_A, H_A * DK_A), F32),
                        pltpu.VMEM((tm, D_MODEL), BF16)],
        compiler_params=pltpu.CompilerParams(dimension_semantics=("arbitrary", "arbitrary"),
                                             vmem_limit_bytes=VMEM_LIMIT),
        name="even_mixer",
    )(x, st_t, buf32, *weights)


def _odd_kernel(x_ref, h0_ref, st0_ref, buf0_ref, win_ref, wcv_ref, bcv_ref, wg_ref, bg_ref, lam_ref, alog_ref,
                dtb_ref, gdl_ref, wout_ref, lng_ref, lnb_ref, bd_ref, bdk_ref, tri_ref, rest_ref,
                y_ref, hnew_ref, stnew_ref, bufnew_ref,
                cbuf_s, rest_s, cv_s, a_s, bx_s, hrow_s, state_s, cat_s, *, tm):
    t = pl.program_id(1)
    nt = pl.num_programs(1)

    @pl.when(t == 0)
    def _():
        hrow_s[...] = h0_ref[0]
        state_s[...] = st0_ref[0]
        cbuf_s[0:8, :] = buf0_ref[0]

    x = x_ref[0]
    xb = x.astype(BF16)
    cbuf_s[8:8 + tm, :] = _dot(xb, win_ref[:, O_CONV:O_GC])
    rest_s[...] = _dot(xb, win_ref[:, O_GC:O_END])

    rb = 32
    for i in range(tm // rb):
        for cb in range(CONV_ODD // D_C):
            cs = slice(cb * D_C, (cb + 1) * D_C)
            acc = jnp.broadcast_to(bcv_ref[:, cs], (rb, D_C))
            for j in range(W_S):
                acc = acc + wcv_ref[j:j + 1, cs] * cbuf_s[i * rb + 5 + j:i * rb + 5 + j + rb, cs]
            cv_s[i * rb:(i + 1) * rb, cs] = acc if cb == 0 else _silu(acc)
    cbuf_s[0:8, :] = cbuf_s[tm:tm + 8, :]

    xc = cv_s[:, 0:D_C]
    gates = _dot(xc.astype(BF16), wg_ref[...]) + bg_ref[...]
    log_a = LRU_C * _sigmoid(gates[:, 0:D_C]) * (-_softplus(-lam_ref[...]))
    av = jnp.exp(log_a)
    one_m_a2 = -jnp.tanh(log_a) * (jnp.exp(2.0 * log_a) + 1.0)
    bv = jnp.sqrt(one_m_a2) * (_sigmoid(gates[:, D_C:2 * D_C]) * xc)
    sub = lax.broadcasted_iota(jnp.int32, (tm, D_C), 0) % 8
    for d in (1, 2, 4):
        keep = sub >= d
        a_up = jnp.where(keep, pltpu.roll(av, d, 0), 1.0)
        b_up = jnp.where(keep, pltpu.roll(bv, d, 0), 0.0)
        bv = av * b_up + bv
        av = av * a_up
    a_s[...] = av
    bx_s[...] = bv

    def lru_group(g, h):
        rows = pl.ds(pl.multiple_of(g * 8, 8), 8)
        hg = a_s[rows, :] * h + bx_s[rows, :]
        bx_s[rows, :] = hg
        return jnp.broadcast_to(hg[7:8, :], (8, D_C))

    hlast = lax.fori_loop(0, tm // 8, lru_group, jnp.broadcast_to(hrow_s[...], (8, D_C)))
    hrow_s[...] = hlast[0:1, :]
    cat_s[:, 0:D_C] = (bx_s[...] * jax.nn.gelu(rest_s[:, 0:D_C])).astype(BF16)

    nc = tm // CHUNK
    chunks = [slice(c * CHUNK, (c + 1) * CHUNK) for c in range(nc)]
    heads = [slice(h * CHUNK, (h + 1) * CHUNK) for h in range(H_D)]
    row = lax.broadcasted_iota(jnp.int32, (tm, H_D * CHUNK), 0) % CHUNK
    col = lax.broadcasted_iota(jnp.int32, (tm, H_D * CHUNK), 1) % CHUNK
    causal = (row >= col).astype(F32)
    strict = (row > col).astype(F32)
    eye = (row == col).astype(F32)
    qoff, koff, voff = D_C, D_C + H_D * DK_D, D_C + 2 * H_D * DK_D

    def per_chunk(fn):
        return jnp.concatenate([fn(cs) for cs in chunks], axis=0)

    def blockdiag(m):
        return jnp.tile(m, (H_D, 1)) * bd_ref[...]

    ba = rest_s[:, O_BA - O_GC:O_END - O_GC]
    beta_all = _sigmoid(ba)
    g_all = -jnp.exp(alog_ref[...]) * _softplus(ba + dtb_ref[...])
    gcum_all = _split_dot(tri_ref[...], g_all)
    grest_all = _split_dot(rest_ref[...], g_all)
    gb = jnp.concatenate([jnp.broadcast_to(g_all[:, H_D + h:H_D + h + 1], (tm, CHUNK)) for h in range(H_D)], axis=1)
    edm = jnp.exp(_split_dot(tri_ref[...], gb * strict))
    kn_l, kb_l, qn_l, rhs_l, qg_l, kg_l, eg_l = [], [], [], [], [], [], []
    for h in range(H_D):
        qh = cv_s[:, qoff + h * DK_D:qoff + (h + 1) * DK_D]
        kh = cv_s[:, koff + h * DK_D:koff + (h + 1) * DK_D]
        vh = cv_s[:, voff + h * DV_D:voff + (h + 1) * DV_D]
        qn = qh * lax.rsqrt(jnp.sum(qh * qh, axis=-1, keepdims=True) + 1e-6) * (DK_D ** -0.5)
        kn = kh * lax.rsqrt(jnp.sum(kh * kh, axis=-1, keepdims=True) + 1e-6)
        beta = beta_all[:, h:h + 1]
        gc = gcum_all[:, H_D + h:H_D + h + 1]
        gr = grest_all[:, H_D + h:H_D + h + 1]
        egc = jnp.exp(gc)
        kb = kn * beta
        kn_l.append(kn.astype(BF16))
        kb_l.append(kb.astype(BF16))
        qn_l.append(qn.astype(BF16))
        rhs_l.append(jnp.concatenate([vh * beta, kb * egc], axis=-1).astype(BF16))
        qg_l.append((qn * egc).astype(BF16))
        kg_l.append((kn * jnp.exp(gr)).astype(BF16))
        eg_l.append(jnp.exp(gc + gr))
    kb_all = jnp.concatenate(kb_l, axis=1)
    qn_all = jnp.concatenate(qn_l, axis=1)
    kn_all = jnp.concatenate(kn_l, axis=1)
    aq = [_dot_nt(jnp.concatenate([kb_all[cs], qn_all[cs]], axis=0), jnp.tile(kn_all[cs], (H_D, 1)) * bdk_ref[...])
          for cs in chunks]
    n = -(jnp.concatenate([a[0:CHUNK] for a in aq], axis=0) * edm * strict)
    attn = (jnp.concatenate([a[CHUNK:2 * CHUNK] for a in aq], axis=0) * edm * causal).astype(BF16)
    tt = eye + n
    pw = n
    for _ in range(5):
        pwb = pw.astype(BF16)
        pw = per_chunk(lambda cs: _dot(pwb[cs], blockdiag(pwb[cs])))
        pwb2 = pw.astype(BF16)
        ttb = tt.astype(BF16)
        tt = tt + per_chunk(lambda cs: _dot(pwb2[cs], blockdiag(ttb[cs])))
    ttb = tt.astype(BF16)
    sol = [[_dot(ttb[cs, hs], rhs_l[h][cs]) for h, hs in enumerate(heads)] for cs in chunks]

    st = [state_s[h] for h in range(H_D)]
    o_l = [[] for _ in range(H_D)]
    for c, cs in enumerate(chunks):
        stb = [s.astype(BF16) for s in st]
        ws = [_dot(jnp.concatenate([sol[c][h][:, DV_D:].astype(BF16), qg_l[h][cs]], axis=0), stb[h])
              for h in range(H_D)]
        vnb = [(sol[c][h][:, 0:DV_D] - ws[h][0:CHUNK]).astype(BF16) for h in range(H_D)]
        for h in range(H_D):
            o_l[h].append(ws[h][CHUNK:2 * CHUNK] + _dot(attn[cs, heads[h]], vnb[h]))
        st = [eg_l[h][c * CHUNK:c * CHUNK + 1, :] * st[h] + _dot_tn(kg_l[h][cs], vnb[h]) for h in range(H_D)]
    for h in range(H_D):
        state_s[h] = st[h]
        o = jnp.concatenate(o_l[h], axis=0)
        zg = rest_s[:, O_Z - O_GC + h * DV_D:O_Z - O_GC + (h + 1) * DV_D]
        od = _rmsnorm(o) * gdl_ref[:, h * DV_D:(h + 1) * DV_D] * _silu(zg)
        cat_s[:, D_C + h * DV_D:D_C + (h + 1) * DV_D] = od.astype(BF16)

    y = _dot(cat_s[...], wout_ref[...])
    y_ref[0] = _layernorm(ALPHA * x + y, lng_ref[...], lnb_ref[...])

    @pl.when(t == nt - 1)
    def _():
        hnew_ref[0] = hrow_s[...]
        stnew_ref[0] = state_s[...]
        bufnew_ref[0] = cbuf_s[0:8, :]


def _chunk_sum_matrices(tm):
    r = np.arange(tm)
    same = (r[:, None] // CHUNK) == (r[None, :] // CHUNK)
    tri = same & (r[None, :] <= r[:, None])
    rest = same & (r[None, :] > r[:, None])
    return jnp.asarray(tri.astype(np.float32), BF16), jnp.asarray(rest.astype(np.float32), BF16)


def _odd_layer(x, h0, st0, buf8, w, tm):
    bsz, L, _ = x.shape
    nt = L // tm
    full = lambda a: pl.BlockSpec(a.shape, lambda b, t: (0,) * a.ndim)
    weights = (w['win'], w['wcv'], w['bcv'], w['wg'], w['bg'], w['lam'], w['alog'], w['dtb'], w['gdl'], w['wout'],
               w['lng'], w['lnb'], w['bd'], w['bdk'], *_chunk_sum_matrices(tm))
    return pl.pallas_call(
        functools.partial(_odd_kernel, tm=tm),
        grid=(bsz, nt),
        in_specs=[pl.BlockSpec((1, tm, D_MODEL), lambda b, t: (b, t, 0)),
                  pl.BlockSpec((1, 1, D_C), lambda b, t: (b, 0, 0)),
                  pl.BlockSpec((1, H_D, DK_D, DV_D), lambda b, t: (b, 0, 0, 0)),
                  pl.BlockSpec((1, 8, CONV_ODD), lambda b, t: (b, 0, 0))] + [full(a) for a in weights],
        out_specs=[pl.BlockSpec((1, tm, D_MODEL), lambda b, t: (b, t, 0)),
                   pl.BlockSpec((1, 1, D_C), lambda b, t: (b, 0, 0)),
                   pl.BlockSpec((1, H_D, DK_D, DV_D), lambda b, t: (b, 0, 0, 0)),
                   pl.BlockSpec((1, 8, CONV_ODD), lambda b, t: (b, 0, 0))],
        out_shape=[jax.ShapeDtypeStruct((bsz, L, D_MODEL), F32),
                   jax.ShapeDtypeStruct((bsz, 1, D_C), F32),
                   jax.ShapeDtypeStruct((bsz, H_D, DK_D, DV_D), F32),
                   jax.ShapeDtypeStruct((bsz, 8, CONV_ODD), F32)],
        scratch_shapes=[pltpu.VMEM((tm + 8, CONV_ODD), F32),
                        pltpu.VMEM((tm, O_END - O_GC), F32),
                        pltpu.VMEM((tm, CONV_ODD), F32),
                        pltpu.VMEM((tm, D_C), F32),
                        pltpu.VMEM((tm, D_C), F32),
                        pltpu.VMEM((1, D_C), F32),
                        pltpu.VMEM((H_D, DK_D, DV_D), F32),
                        pltpu.VMEM((tm, D_MODEL), BF16)],
        compiler_params=pltpu.CompilerParams(dimension_semantics=("arbitrary", "arbitrary"),
                                             vmem_limit_bytes=VMEM_LIMIT),
        name="odd_mixer",
    )(x, h0, st0, buf8, *weights)


def _ffn_kernel(x_ref, buf0_ref, wup_ref, wdw_ref, bdw_ref, wdown_ref, lng_ref, lnb_ref,
                y_ref, bufnew_ref, hbuf_s, *, tm):
    t = pl.program_id(1)
    nt = pl.num_programs(1)

    @pl.when(t == 0)
    def _():
        hbuf_s[0:8, :] = buf0_ref[0]

    x = x_ref[0]
    xb = x.astype(BF16)
    acc = jnp.zeros((tm, D_MODEL), F32)
    for c0 in range(0, D_FF, FFN_COLS):
        cs = slice(c0, min(c0 + FFN_COLS, D_FF))
        hg = _dot(xb, wup_ref[:, cs])
        hbuf_s[8:8 + tm, cs] = hg
        cv = (bdw_ref[:, cs] + wdw_ref[0:1, cs] * hbuf_s[6:6 + tm, cs] + wdw_ref[1:2, cs] * hbuf_s[7:7 + tm, cs]
              + wdw_ref[2:3, cs] * hg)
        hv = _dot(xb, wup_ref[:, D_FF + cs.start:D_FF + cs.stop])
        act = jax.nn.gelu(cv) * hv
        acc = acc + _dot(act.astype(BF16), wdown_ref[cs, :])
    hbuf_s[0:8, :] = hbuf_s[tm:tm + 8, :]
    y_ref[0] = _layernorm(ALPHA * x + acc, lng_ref[...], lnb_ref[...])

    @pl.when(t == nt - 1)
    def _():
        bufnew_ref[0] = hbuf_s[0:8, :]


def _ffn_layer(x, buf8, w, tm):
    bsz, L, _ = x.shape
    nt = L // tm
    full = lambda a: pl.BlockSpec(a.shape, lambda b, t: (0,) * a.ndim)
    weights = (w['wup'], w['wdw'], w['bdw'], w['wdown'], w['lng'], w['lnb'])
    return pl.pallas_call(
        functools.partial(_ffn_kernel, tm=tm),
        grid=(bsz, nt),
        in_specs=[pl.BlockSpec((1, tm, D_MODEL), lambda b, t: (b, t, 0)),
                  pl.BlockSpec((1, 8, D_FF), lambda b, t: (b, 0, 0))] + [full(a) for a in weights],
        out_specs=[pl.BlockSpec((1, tm, D_MODEL), lambda b, t: (b, t, 0)),
                   pl.BlockSpec((1, 8, D_FF), lambda b, t: (b, 0, 0))],
        out_shape=[jax.ShapeDtypeStruct((bsz, L, D_MODEL), F32),
                   jax.ShapeDtypeStruct((bsz, 8, D_FF), F32)],
        scratch_shapes=[pltpu.VMEM((tm + 8, D_FF), F32)],
        compiler_params=pltpu.CompilerParams(dimension_semantics=("arbitrary", "arbitrary"),
                                             vmem_limit_bytes=VMEM_LIMIT),
        name="conv_ffn",
    )(x, buf8, *weights)


def _row(v):
    return v.reshape(1, -1).astype(F32)


def _pad_rows(a, rows):
    return jnp.pad(a, ((0, rows - a.shape[0]), (0, 0)))


def _block_diag(w):
    h, d, _ = w.shape
    return jnp.einsum('hij,hg->higj', w, jnp.eye(h, dtype=w.dtype)).reshape(h * d, h * d)


def _prep_even(i, we_in, we_lr, be_lr, ge_gla, we_dw, be_dw, ge_cn, be_cn, we_out, ln1_g, ln1_b, l):
    w_in = we_in[i]
    lr0 = 2 * H_A * DK_A + 2 * H_A * DV_A
    win = jnp.concatenate([w_in[:, :lr0], w_in[:, lr0 + R_A:], w_in[:, lr0:lr0 + R_A],
                           jnp.zeros((D_MODEL, LANE - R_A), w_in.dtype)], axis=1).astype(BF16)
    return dict(win=win, wlr=_pad_rows(we_lr[i], LANE).astype(BF16), blr=_row(be_lr[i]), ggla=_row(ge_gla[i]),
                wdw=_pad_rows(we_dw[i], 32), bdw=_row(be_dw[i]), gcn=_row(ge_cn[i]), bcn=_row(be_cn[i]),
                wout=we_out[i].astype(BF16), lng=_row(ln1_g[l]), lnb=_row(ln1_b[l]),
                bdk=jnp.asarray(_block_mask(H_A * CHUNK, H_A * DK_A, CHUNK, DK_A), BF16),
                bdv=jnp.asarray(_block_mask(H_A * CHUNK, H_A * DV_A, CHUNK, DV_A), BF16),
                bdst=jnp.asarray(_block_mask(H_A * DV_A, H_A * DK_A, DV_A, DK_A), F32))


def _prep_odd(i, wo_in, wo_conv, bo_conv, wo_rg, bo_rg, wo_ig, bo_ig, lam_lru, a_log, dt_bias, go_delta, wo_out,
              ln1_g, ln1_b, l):
    w_in = wo_in[i]
    win = jnp.concatenate([w_in, jnp.zeros((D_MODEL, O_END - w_in.shape[1]), w_in.dtype)], axis=1).astype(BF16)
    head_row = lambda v: jnp.pad(v.astype(F32), (H_D, LANE - 2 * H_D)).reshape(1, LANE)
    return dict(win=win, wcv=_pad_rows(wo_conv[i], 8), bcv=_row(bo_conv[i]),
                wg=jnp.concatenate([_block_diag(wo_rg[i]), _block_diag(wo_ig[i])], axis=1).astype(BF16),
                bg=_row(jnp.concatenate([bo_rg[i], bo_ig[i]])), lam=_row(lam_lru[i]),
                alog=head_row(a_log[i]), dtb=head_row(dt_bias[i]), gdl=_row(go_delta[i]),
                wout=wo_out[i].astype(BF16), lng=_row(ln1_g[l]), lnb=_row(ln1_b[l]),
                bd=jnp.asarray(_block_mask(H_D * CHUNK, H_D * CHUNK, CHUNK, CHUNK), BF16),
                bdk=jnp.asarray(_block_mask(H_D * CHUNK, H_D * DK_D, CHUNK, DK_D), BF16))


def _prep_ffn(l, w_up, w_fdw, b_fdw, w_down, ln2_g, ln2_b):
    return dict(wup=w_up[l].astype(BF16), wdw=_pad_rows(w_fdw[l], 8), bdw=_row(b_fdw[l]),
                wdown=w_down[l].astype(BF16), lng=_row(ln2_g[l]), lnb=_row(ln2_b[l]))


def _front_pad(buf, rows):
    return jnp.pad(buf, ((0, 0), (rows - buf.shape[1], 0), (0, 0)))


def _gla_state_to_blockdiag(s):
    s_t = jnp.swapaxes(s, 2, 3)
    return jnp.concatenate([jnp.pad(s_t[:, h], ((0, 0), (0, 0), (h * DK_A, (H_A - 1 - h) * DK_A)))
                            for h in range(H_A)], axis=1)


def _gla_state_from_blockdiag(s_bd):
    bsz = s_bd.shape[0]
    s5 = s_bd.reshape(bsz, H_A, DV_A, H_A, DK_A)
    return jnp.stack([jnp.swapaxes(s5[:, h, :, h, :], 1, 2) for h in range(H_A)], axis=1)


def _trunk(x, states, mix_w, ffn_w):
    L = x.shape[1]
    tm = min(L, TIME_TILE)
    new_states = []
    for l in range(DEPTH):
        st = states[l]
        if l % 2 == 0:
            x, s_bd, buf = _even_layer(x, _gla_state_to_blockdiag(st[0]), _front_pad(st[1], 32), mix_w[l], tm)
            mix_new = (_gla_state_from_blockdiag(s_bd), buf[:, 32 - (W_B - 1):])
        else:
            x, h, s, buf = _odd_layer(x, st[0][:, None, :], st[1], _front_pad(st[2], 8), mix_w[l], tm)
            mix_new = (h[:, 0], s, buf[:, 8 - (W_S - 1):])
        x, fbuf = _ffn_layer(x, _front_pad(st[-1], 8), ffn_w[l], tm)
        new_states.append((*mix_new, fbuf[:, 8 - (W_F - 1):]))
    return x, new_states


def _zero_states(bsz):
    z = lambda *s: jnp.zeros((bsz,) + s, F32)
    return [(z(H_A, DK_A, DV_A), z(W_B - 1, D_B), z(W_F - 1, D_FF)) if l % 2 == 0 else
            (z(D_C), z(H_D, DK_D, DV_D), z(W_S - 1, CONV_ODD), z(W_F - 1, D_FF)) for l in range(DEPTH)]


def kernel(x_prompt, x_sample, state_l0_gla, cache_l0_dwconv, cache_l0_ffn, state_l1_lru, state_l1_delta, cache_l1_conv, cache_l1_ffn, state_l2_gla, cache_l2_dwconv, cache_l2_ffn, state_l3_lru, state_l3_delta, cache_l3_conv, cache_l3_ffn, we_in, we_lr, be_lr, ge_gla, we_dw, be_dw, ge_cn, be_cn, we_out, wo_in, wo_conv, bo_conv, wo_rg, bo_rg, wo_ig, bo_ig, lam_lru, a_log, dt_bias, go_delta, wo_out, w_up, w_fdw, b_fdw, w_down, ln1_g, ln1_b, ln2_g, ln2_b):
    mix_w = []
    for l in range(DEPTH):
        if l % 2 == 0:
            mix_w.append(_prep_even(l // 2, we_in, we_lr, be_lr, ge_gla, we_dw, be_dw, ge_cn, be_cn, we_out,
                                    ln1_g, ln1_b, l))
        else:
            mix_w.append(_prep_odd(l // 2, wo_in, wo_conv, bo_conv, wo_rg, bo_rg, wo_ig, bo_ig, lam_lru, a_log,
                                   dt_bias, go_delta, wo_out, ln1_g, ln1_b, l))
    ffn_w = [_prep_ffn(l, w_up, w_fdw, b_fdw, w_down, ln2_g, ln2_b) for l in range(DEPTH)]
    y_prompt, new_p = _trunk(x_prompt, _zero_states(x_prompt.shape[0]), mix_w, ffn_w)
    sample_states = [(state_l0_gla, cache_l0_dwconv, cache_l0_ffn),
                     (state_l1_lru, state_l1_delta, cache_l1_conv, cache_l1_ffn),
                     (state_l2_gla, cache_l2_dwconv, cache_l2_ffn),
                     (state_l3_lru, state_l3_delta, cache_l3_conv, cache_l3_ffn)]
    y_sample, new_s = _trunk(x_sample, sample_states, mix_w, ffn_w)
    flat = lambda ns: [a for layer in ns for a in layer]
    return (y_prompt, y_sample, *flat(new_p), *flat(new_s))
```

```python
import functools

import numpy as np
import jax
import jax.numpy as jnp
from jax import lax
from jax.experimental import pallas as pl
from jax.experimental.pallas import tpu as pltpu

F32 = jnp.float32
BF16 = jnp.bfloat16

D_MODEL = 1024
DEPTH = 4
CHUNK = 64
H_A, DK_A, DV_A, R_A, TAU_A = 4, 64, 128, 16, 16.0
D_B, W_B = 512, 31
D_C, H_C, DH_C, LRU_C = 512, 8, 64, 8.0
H_D, DK_D, DV_D, W_S = 4, 128, 128, 4
D_FF, W_F = 2688, 3
ALPHA = (2 * DEPTH) ** 0.25
EPS = 1e-5
CONV_ODD = D_C + 2 * H_D * DK_D + H_D * DV_D

LANE = 128
EVEN_TILE = 256
ODD_TILE = 512
FFN_TILE = 512
VMEM_LIMIT = 56 * 1024 * 1024
MXU_TILE = 256
FFN_COLS = 5 * MXU_TILE
GLA_LEVELS = 6

E_Q, E_K, E_V, E_G, E_GA, E_GB, E_LR, E_END = 0, 256, 512, 1024, 1536, 2048, 2560, 2688
O_CONV, O_GC, O_Z, O_BA, O_END = 0, 2048, 2560, 3072, 3200


def _dot(a, b):
    return jnp.dot(a, b, preferred_element_type=F32)


def _dot_nt(a, b):
    return lax.dot_general(a, b, (((1,), (1,)), ((), ())), preferred_element_type=F32)


def _dot_tn(a, b):
    return lax.dot_general(a, b, (((0,), (0,)), ((), ())), preferred_element_type=F32)


def _split_dot(m, x):
    hi = x.astype(BF16)
    lo = (x - hi.astype(F32)).astype(BF16)
    return _dot(m, hi) + _dot(m, lo)


def _sigmoid(x):
    return jax.nn.sigmoid(x)


def _silu(x):
    return x * jax.nn.sigmoid(x)


def _softplus(x):
    return jnp.maximum(x, 0.0) + jnp.log(1.0 + jnp.exp(-jnp.abs(x)))


def _layernorm(x, g, b):
    mu = jnp.mean(x, axis=-1, keepdims=True)
    xc = x - mu
    var = jnp.mean(xc * xc, axis=-1, keepdims=True)
    return xc * lax.rsqrt(var + EPS) * g + b


def _rmsnorm(x):
    return x * lax.rsqrt(jnp.mean(x * x, axis=-1, keepdims=True) + EPS)


def _block_mask(rows, cols, rblk, cblk):
    r = np.arange(rows)[:, None] // rblk
    c = np.arange(cols)[None, :] // cblk
    return (r == c).astype(np.float32)


def _gla_constants(tm):
    c = CHUNK
    r = np.arange(c)
    tri = (r[None, :] <= r[:, None]).astype(np.float32)
    rest = (r[None, :] > r[:, None]).astype(np.float32)
    mats = [tri, rest]
    masks = [np.eye(c, dtype=np.float32)]
    for lvl in range(GLA_LEVELS):
        m = (c // 2) >> lvl
        anchor = (r // (2 * m)) * (2 * m) + m - 1
        mats.append(tri - tri[anchor])
        same = (r[:, None] // (2 * m)) == (r[None, :] // (2 * m))
        masks.append((same & ((r[:, None] % (2 * m)) >= m) & ((r[None, :] % (2 * m)) < m)).astype(np.float32))
    nc = tm // c
    call = np.concatenate([np.kron(np.eye(nc, dtype=np.float32), m) for m in mats], 0)
    return jnp.asarray(call, BF16), jnp.asarray(np.tile(np.stack(masks, 0), (1, nc, H_A)), F32)


def _even_kernel(x_ref, st0_ref, buf0_ref, win_ref, wlr_ref, blr_ref, ggla_ref, wdw_ref, bdw_ref, gcn_ref, bcn_ref,
                 wout_ref, lng_ref, lnb_ref, call_ref, masks_ref, bdk_ref, bdv_ref, bdst_ref,
                 y_ref, stnew_ref, bufnew_ref,
                 proj_s, shift_s, state_s, cat_s, *, tm):
    t = pl.program_id(1)
    nt = pl.num_programs(1)

    @pl.when(t == 0)
    def _():
        state_s[...] = st0_ref[0]
        shift_s[0, 0:32, :] = buf0_ref[0]

    x = x_ref[0]
    proj_s[...] = _dot(x.astype(BF16), win_ref[...])

    shift_s[0, 32:32 + tm, :] = proj_s[:, E_GA:E_GB] * _sigmoid(proj_s[:, E_GB:E_LR])
    for s in range(1, 8):
        shift_s[s, 0:tm + 24, :] = shift_s[0, s:s + tm + 24, :]
    rb = 32
    for i in range(tm // rb):
        acc = jnp.broadcast_to(bdw_ref[...], (rb, D_B))
        for j in range(W_B):
            off = i * rb + 2 + j
            acc = acc + wdw_ref[j:j + 1, :] * shift_s[off % 8, off - off % 8:off - off % 8 + rb, :]
        ob = _silu(_layernorm(acc, gcn_ref[...], bcn_ref[...]))
        cat_s[i * rb:(i + 1) * rb, D_B:2 * D_B] = ob.astype(BF16)
    shift_s[0, 0:32, :] = shift_s[0, tm:tm + 32, :]

    nc = tm // CHUNK
    chunks = [slice(c * CHUNK, (c + 1) * CHUNK) for c in range(nc)]

    def per_chunk(fn):
        return jnp.concatenate([fn(cs) for cs in chunks], axis=0)

    q = proj_s[:, E_Q:E_K] * (DK_A ** -0.5)
    k = proj_s[:, E_K:E_V]
    vb = proj_s[:, E_V:E_G].astype(BF16)
    z = _dot(proj_s[:, E_LR:E_END].astype(BF16), wlr_ref[...]) + blr_ref[...]
    la = -_softplus(-z) * (1.0 / TAU_A)
    e_all = _split_dot(call_ref[...], la)
    bcum = e_all[0:tm]
    brest = e_all[tm:2 * tm]
    qe = (q * jnp.exp(bcum)).astype(BF16)
    kr = (k * jnp.exp(brest)).astype(BF16)
    dlast = jnp.exp(bcum + brest)
    kv = [bdst_ref[...] * _dot_tn(vb[cs], kr[cs]) for cs in chunks]
    p = jnp.zeros((tm, H_A * CHUNK), F32)
    for lvl in range(GLA_LEVELS + 1):
        if lvl == 0:
            qf, kf = q.astype(BF16), k.astype(BF16)
        else:
            f = jnp.exp(-jnp.abs(e_all[(1 + lvl) * tm:(2 + lvl) * tm]))
            qf, kf = (q * f).astype(BF16), (k * f).astype(BF16)
        p = p + masks_ref[lvl] * per_chunk(lambda cs: _dot_nt(qf[cs], jnp.tile(kf[cs], (H_A, 1)) * bdk_ref[...]))
    pb = p.astype(BF16)
    o_intra = [_dot(pb[cs], jnp.tile(vb[cs], (H_A, 1)) * bdv_ref[...]) for cs in chunks]
    st = state_s[...]
    sts = []
    for c in range(nc):
        sts.append(st.astype(BF16))
        st = st * dlast[c * CHUNK:c * CHUNK + 1, :] + kv[c]
    state_s[...] = st
    o = jnp.concatenate([_dot_nt(qe[cs], sts[c]) + o_intra[c] for c, cs in enumerate(chunks)], axis=0)
    for h in range(H_A):
        hs = slice(h * DV_A, (h + 1) * DV_A)
        oa = _rmsnorm(o[:, hs]) * ggla_ref[:, hs] * _silu(proj_s[:, E_G + h * DV_A:E_G + (h + 1) * DV_A])
        cat_s[:, hs] = oa.astype(BF16)

    y = _dot(cat_s[...], wout_ref[...])
    y_ref[0] = _layernorm(ALPHA * x + y, lng_ref[...], lnb_ref[...])

    @pl.when(t == nt - 1)
    def _():
        stnew_ref[0] = state_s[...]
        bufnew_ref[0] = shift_s[0, 0:32, :]


def _even_layer(x, st_t, buf32, w, tm):
    bsz, L, _ = x.shape
    nt = L // tm
    full = lambda a: pl.BlockSpec(a.shape, lambda b, t: (0,) * a.ndim)
    weights = (w['win'], w['wlr'], w['blr'], w['ggla'], w['wdw'], w['bdw'], w['gcn'], w['bcn'], w['wout'],
               w['lng'], w['lnb'], *_gla_constants(tm), w['bdk'], w['bdv'], w['bdst'])
    return pl.pallas_call(
        functools.partial(_even_kernel, tm=tm),
        grid=(bsz, nt),
        in_specs=[pl.BlockSpec((1, tm, D_MODEL), lambda b, t: (b, t, 0)),
                  pl.BlockSpec((1, H_A * DV_A, H_A * DK_A), lambda b, t: (b, 0, 0)),
                  pl.BlockSpec((1, 32, D_B), lambda b, t: (b, 0, 0))] + [full(a) for a in weights],
        out_specs=[pl.BlockSpec((1, tm, D_MODEL), lambda b, t: (b, t, 0)),
                   pl.BlockSpec((1, H_A * DV_A, H_A * DK_A), lambda b, t: (b, 0, 0)),
                   pl.BlockSpec((1, 32, D_B), lambda b, t: (b, 0, 0))],
        out_shape=[jax.ShapeDtypeStruct((bsz, L, D_MODEL), F32),
                   jax.ShapeDtypeStruct((bsz, H_A * DV_A, H_A * DK_A), F32),
                   jax.ShapeDtypeStruct((bsz, 32, D_B), F32)],
        scratch_shapes=[pltpu.VMEM((tm, E_END), F32),
                        pltpu.VMEM((8, tm + 32, D_B), F32),
                        pltpu.VMEM((H_A * DV_A, H_A * DK_A), F32),
                        pltpu.VMEM((tm, D_MODEL), BF16)],
        compiler_params=pltpu.CompilerParams(dimension_semantics=("arbitrary", "arbitrary"),
                                             vmem_limit_bytes=VMEM_LIMIT),
        name="even_mixer",
    )(x, st_t, buf32, *weights)


def _odd_kernel(x_ref, h0_ref, st0_ref, buf0_ref, win_ref, wcv_ref, bcv_ref, wg_ref, bg_ref, lam_ref, alog_ref,
                dtb_ref, gdl_ref, wout_ref, lng_ref, lnb_ref, bd_ref, bdk_ref, tri_ref, rest_ref,
                y_ref, hnew_ref, stnew_ref, bufnew_ref,
                cbuf_s, rest_s, cv_s, a_s, bx_s, hrow_s, state_s, cat_s, *, tm):
    t = pl.program_id(1)
    nt = pl.num_programs(1)

    @pl.when(t == 0)
    def _():
        hrow_s[...] = h0_ref[0]
        state_s[...] = st0_ref[0]
        cbuf_s[0:8, :] = buf0_ref[0]

    x = x_ref[0]
    xb = x.astype(BF16)
    cbuf_s[8:8 + tm, :] = _dot(xb, win_ref[:, O_CONV:O_GC])
    rest_s[...] = _dot(xb, win_ref[:, O_GC:O_END])

    rb = 32
    for i in range(tm // rb):
        for cb in range(CONV_ODD // D_C):
            cs = slice(cb * D_C, (cb + 1) * D_C)
            acc = jnp.broadcast_to(bcv_ref[:, cs], (rb, D_C))
            for j in range(W_S):
                acc = acc + wcv_ref[j:j + 1, cs] * cbuf_s[i * rb + 5 + j:i * rb + 5 + j + rb, cs]
            cv_s[i * rb:(i + 1) * rb, cs] = acc if cb == 0 else _silu(acc)
    cbuf_s[0:8, :] = cbuf_s[tm:tm + 8, :]

    xc = cv_s[:, 0:D_C]
    gates = _dot(xc.astype(BF16), wg_ref[...]) + bg_ref[...]
    log_a = LRU_C * _sigmoid(gates[:, 0:D_C]) * (-_softplus(-lam_ref[...]))
    av = jnp.exp(log_a)
    one_m_a2 = -jnp.tanh(log_a) * (jnp.exp(2.0 * log_a) + 1.0)
    bv = jnp.sqrt(one_m_a2) * (_sigmoid(gates[:, D_C:2 * D_C]) * xc)
    sub = lax.broadcasted_iota(jnp.int32, (tm, D_C), 0) % 8
    for d in (1, 2, 4):
        keep = sub >= d
        a_up = jnp.where(keep, pltpu.roll(av, d, 0), 1.0)
        b_up = jnp.where(keep, pltpu.roll(bv, d, 0), 0.0)
        bv = av * b_up + bv
        av = av * a_up
    a_s[...] = av
    bx_s[...] = bv

    def lru_group(g, h):
        rows = pl.ds(pl.multiple_of(g * 8, 8), 8)
        hg = a_s[rows, :] * h + bx_s[rows, :]
        bx_s[rows, :] = hg
        return jnp.broadcast_to(hg[7:8, :], (8, D_C))

    hlast = lax.fori_loop(0, tm // 8, lru_group, jnp.broadcast_to(hrow_s[...], (8, D_C)))
    hrow_s[...] = hlast[0:1, :]
    cat_s[:, 0:D_C] = (bx_s[...] * jax.nn.gelu(rest_s[:, 0:D_C])).astype(BF16)

    nc = tm // CHUNK
    chunks = [slice(c * CHUNK, (c + 1) * CHUNK) for c in range(nc)]
    heads = [slice(h * CHUNK, (h + 1) * CHUNK) for h in range(H_D)]
    row = lax.broadcasted_iota(jnp.int32, (tm, H_D * CHUNK), 0) % CHUNK
    col = lax.broadcasted_iota(jnp.int32, (tm, H_D * CHUNK), 1) % CHUNK
    causal = (row >= col).astype(F32)
    strict = (row > col).astype(F32)
    eye = (row == col).astype(F32)
    qoff, koff, voff = D_C, D_C + H_D * DK_D, D_C + 2 * H_D * DK_D

    def per_chunk(fn):
        return jnp.concatenate([fn(cs) for cs in chunks], axis=0)

    def blockdiag(m):
        return jnp.tile(m, (H_D, 1)) * bd_ref[...]

    ba = rest_s[:, O_BA - O_GC:O_END - O_GC]
    beta_all = _sigmoid(ba)
    g_all = -jnp.exp(alog_ref[...]) * _softplus(ba + dtb_ref[...])
    gcum_all = _split_dot(tri_ref[...], g_all)
    grest_all = _split_dot(rest_ref[...], g_all)
    gb = jnp.concatenate([jnp.broadcast_to(g_all[:, H_D + h:H_D + h + 1], (tm, CHUNK)) for h in range(H_D)], axis=1)
    edm = jnp.exp(_split_dot(tri_ref[...], gb * strict))
    kn_l, kb_l, qn_l, rhs_l, qg_l, kg_l, eg_l = [], [], [], [], [], [], []
    for h in range(H_D):
        qh = cv_s[:, qoff + h * DK_D:qoff + (h + 1) * DK_D]
        kh = cv_s[:, koff + h * DK_D:koff + (h + 1) * DK_D]
        vh = cv_s[:, voff + h * DV_D:voff + (h + 1) * DV_D]
        qn = qh * lax.rsqrt(jnp.sum(qh * qh, axis=-1, keepdims=True) + 1e-6) * (DK_D ** -0.5)
        kn = kh * lax.rsqrt(jnp.sum(kh * kh, axis=-1, keepdims=True) + 1e-6)
        beta = beta_all[:, h:h + 1]
        gc = gcum_all[:, H_D + h:H_D + h + 1]
        gr = grest_all[:, H_D + h:H_D + h + 1]
        egc = jnp.exp(gc)
        kb = kn * beta
        kn_l.append(kn.astype(BF16))
        kb_l.append(kb.astype(BF16))
        qn_l.append(qn.astype(BF16))
        rhs_l.append(jnp.concatenate([vh * beta, kb * egc], axis=-1).astype(BF16))
        qg_l.append((qn * egc).astype(BF16))
        kg_l.append((kn * jnp.exp(gr)).astype(BF16))
        eg_l.append(jnp.exp(gc + gr))
    kb_all = jnp.concatenate(kb_l, axis=1)
    qn_all = jnp.concatenate(qn_l, axis=1)
    kn_all = jnp.concatenate(kn_l, axis=1)
    aq = [_dot_nt(jnp.concatenate([kb_all[cs], qn_all[cs]], axis=0), jnp.tile(kn_all[cs], (H_D, 1)) * bdk_ref[...])
          for cs in chunks]
    n = -(jnp.concatenate([a[0:CHUNK] for a in aq], axis=0) * edm * strict)
    attn = (jnp.concatenate([a[CHUNK:2 * CHUNK] for a in aq], axis=0) * edm * causal).astype(BF16)
    tt = eye + n
    pw = n
    for _ in range(5):
        pwb = pw.astype(BF16)
        pw = per_chunk(lambda cs: _dot(pwb[cs], blockdiag(pwb[cs])))
        pwb2 = pw.astype(BF16)
        ttb = tt.astype(BF16)
        tt = tt + per_chunk(lambda cs: _dot(pwb2[cs], blockdiag(ttb[cs])))
    ttb = tt.astype(BF16)
    sol = [[_dot(ttb[cs, hs], rhs_l[h][cs]) for h, hs in enumerate(heads)] for cs in chunks]

    st = [state_s[h] for h in range(H_D)]
    o_l = [[] for _ in range(H_D)]
    for c, cs in enumerate(chunks):
        stb = [s.astype(BF16) for s in st]
        ws = [_dot(jnp.concatenate([sol[c][h][:, DV_D:].astype(BF16), qg_l[h][cs]], axis=0), stb[h])
              for h in range(H_D)]
        vnb = [(sol[c][h][:, 0:DV_D] - ws[h][0:CHUNK]).astype(BF16) for h in range(H_D)]
        for h in range(H_D):
            o_l[h].append(ws[h][CHUNK:2 * CHUNK] + _dot(attn[cs, heads[h]], vnb[h]))
        st = [eg_l[h][c * CHUNK:c * CHUNK + 1, :] * st[h] + _dot_tn(kg_l[h][cs], vnb[h]) for h in range(H_D)]
    for h in range(H_D):
        state_s[h] = st[h]
        o = jnp.concatenate(o_l[h], axis=0)
        zg = rest_s[:, O_Z - O_GC + h * DV_D:O_Z - O_GC + (h + 1) * DV_D]
        od = _rmsnorm(o) * gdl_ref[:, h * DV_D:(h + 1) * DV_D] * _silu(zg)
        cat_s[:, D_C + h * DV_D:D_C + (h + 1) * DV_D] = od.astype(BF16)

    y = _dot(cat_s[...], wout_ref[...])
    y_ref[0] = _layernorm(ALPHA * x + y, lng_ref[...], lnb_ref[...])

    @pl.when(t == nt - 1)
    def _():
        hnew_ref[0] = hrow_s[...]
        stnew_ref[0] = state_s[...]
        bufnew_ref[0] = cbuf_s[0:8, :]


def _chunk_sum_matrices(tm):
    r = np.arange(tm)
    same = (r[:, None] // CHUNK) == (r[None, :] // CHUNK)
    tri = same & (r[None, :] <= r[:, None])
    rest = same & (r[None, :] > r[:, None])
    return jnp.asarray(tri.astype(np.float32), BF16), jnp.asarray(rest.astype(np.float32), BF16)


def _odd_layer(x, h0, st0, buf8, w, tm):
    bsz, L, _ = x.shape
    nt = L // tm
    full = lambda a: pl.BlockSpec(a.shape, lambda b, t: (0,) * a.ndim)
    weights = (w['win'], w['wcv'], w['bcv'], w['wg'], w['bg'], w['lam'], w['alog'], w['dtb'], w['gdl'], w['wout'],
               w['lng'], w['lnb'], w['bd'], w['bdk'], *_chunk_sum_matrices(tm))
    return pl.pallas_call(
        functools.partial(_odd_kernel, tm=tm),
        grid=(bsz, nt),
        in_specs=[pl.BlockSpec((1, tm, D_MODEL), lambda b, t: (b, t, 0)),
                  pl.BlockSpec((1, 1, D_C), lambda b, t: (b, 0, 0)),
                  pl.BlockSpec((1, H_D, DK_D, DV_D), lambda b, t: (b, 0, 0, 0)),
                  pl.BlockSpec((1, 8, CONV_ODD), lambda b, t: (b, 0, 0))] + [full(a) for a in weights],
        out_specs=[pl.BlockSpec((1, tm, D_MODEL), lambda b, t: (b, t, 0)),
                   pl.BlockSpec((1, 1, D_C), lambda b, t: (b, 0, 0)),
                   pl.BlockSpec((1, H_D, DK_D, DV_D), lambda b, t: (b, 0, 0, 0)),
                   pl.BlockSpec((1, 8, CONV_ODD), lambda b, t: (b, 0, 0))],
        out_shape=[jax.ShapeDtypeStruct((bsz, L, D_MODEL), F32),
                   jax.ShapeDtypeStruct((bsz, 1, D_C), F32),
                   jax.ShapeDtypeStruct((bsz, H_D, DK_D, DV_D), F32),
                   jax.ShapeDtypeStruct((bsz, 8, CONV_ODD), F32)],
        scratch_shapes=[pltpu.VMEM((tm + 8, CONV_ODD), F32),
                        pltpu.VMEM((tm, O_END - O_GC), F32),
                        pltpu.VMEM((tm, CONV_ODD), F32),
                        pltpu.VMEM((tm, D_C), F32),
                        pltpu.VMEM((tm, D_C), F32),
                        pltpu.VMEM((1, D_C), F32),
                        pltpu.VMEM((H_D, DK_D, DV_D), F32),
                        pltpu.VMEM((tm, D_MODEL), BF16)],
        compiler_params=pltpu.CompilerParams(dimension_semantics=("arbitrary", "arbitrary"),
                                             vmem_limit_bytes=VMEM_LIMIT),
        name="odd_mixer",
    )(x, h0, st0, buf8, *weights)


def _ffn_kernel(x_ref, buf0_ref, wup_ref, wdw_ref, bdw_ref, wdown_ref, lng_ref, lnb_ref,
                y_ref, bufnew_ref, hbuf_s, *, tm):
    t = pl.program_id(1)
    nt = pl.num_programs(1)

    @pl.when(t == 0)
    def _():
        hbuf_s[0:8, :] = buf0_ref[0]

    x = x_ref[0]
    xb = x.astype(BF16)
    acc = jnp.zeros((tm, D_MODEL), F32)
    for c0 in range(0, D_FF, FFN_COLS):
        cs = slice(c0, min(c0 + FFN_COLS, D_FF))
        hg = _dot(xb, wup_ref[:, cs])
        hbuf_s[8:8 + tm, cs] = hg
        cv = (bdw_ref[:, cs] + wdw_ref[0:1, cs] * hbuf_s[6:6 + tm, cs] + wdw_ref[1:2, cs] * hbuf_s[7:7 + tm, cs]
              + wdw_ref[2:3, cs] * hg)
        hv = _dot(xb, wup_ref[:, D_FF + cs.start:D_FF + cs.stop])
        act = jax.nn.gelu(cv) * hv
        acc = acc + _dot(act.astype(BF16), wdown_ref[cs, :])
    hbuf_s[0:8, :] = hbuf_s[tm:tm + 8, :]
    y_ref[0] = _layernorm(ALPHA * x + acc, lng_ref[...], lnb_ref[...])

    @pl.when(t == nt - 1)
    def _():
        bufnew_ref[0] = hbuf_s[0:8, :]


def _ffn_layer(x, buf8, w, tm):
    bsz, L, _ = x.shape
    nt = L // tm
    full = lambda a: pl.BlockSpec(a.shape, lambda b, t: (0,) * a.ndim)
    weights = (w['wup'], w['wdw'], w['bdw'], w['wdown'], w['lng'], w['lnb'])
    return pl.pallas_call(
        functools.partial(_ffn_kernel, tm=tm),
        grid=(bsz, nt),
        in_specs=[pl.BlockSpec((1, tm, D_MODEL), lambda b, t: (b, t, 0)),
                  pl.BlockSpec((1, 8, D_FF), lambda b, t: (b, 0, 0))] + [full(a) for a in weights],
        out_specs=[pl.BlockSpec((1, tm, D_MODEL), lambda b, t: (b, t, 0)),
                   pl.BlockSpec((1, 8, D_FF), lambda b, t: (b, 0, 0))],
        out_shape=[jax.ShapeDtypeStruct((bsz, L, D_MODEL), F32),
                   jax.ShapeDtypeStruct((bsz, 8, D_FF), F32)],
        scratch_shapes=[pltpu.VMEM((tm + 8, D_FF), F32)],
        compiler_params=pltpu.CompilerParams(dimension_semantics=("arbitrary", "arbitrary"),
                                             vmem_limit_bytes=VMEM_LIMIT),
        name="conv_ffn",
    )(x, buf8, *weights)


def _row(v):
    return v.reshape(1, -1).astype(F32)


def _pad_rows(a, rows):
    return jnp.pad(a, ((0, rows - a.shape[0]), (0, 0)))


def _block_diag(w):
    h, d, _ = w.shape
    return jnp.einsum('hij,hg->higj', w, jnp.eye(h, dtype=w.dtype)).reshape(h * d, h * d)


def _prep_even(i, we_in, we_lr, be_lr, ge_gla, we_dw, be_dw, ge_cn, be_cn, we_out, ln1_g, ln1_b, l):
    w_in = we_in[i]
    lr0 = 2 * H_A * DK_A + 2 * H_A * DV_A
    win = jnp.concatenate([w_in[:, :lr0], w_in[:, lr0 + R_A:], w_in[:, lr0:lr0 + R_A],
                           jnp.zeros((D_MODEL, LANE - R_A), w_in.dtype)], axis=1).astype(BF16)
    return dict(win=win, wlr=_pad_rows(we_lr[i], LANE).astype(BF16), blr=_row(be_lr[i]), ggla=_row(ge_gla[i]),
                wdw=_pad_rows(we_dw[i], 32), bdw=_row(be_dw[i]), gcn=_row(ge_cn[i]), bcn=_row(be_cn[i]),
                wout=we_out[i].astype(BF16), lng=_row(ln1_g[l]), lnb=_row(ln1_b[l]),
                bdk=jnp.asarray(_block_mask(H_A * CHUNK, H_A * DK_A, CHUNK, DK_A), BF16),
                bdv=jnp.asarray(_block_mask(H_A * CHUNK, H_A * DV_A, CHUNK, DV_A), BF16),
                bdst=jnp.asarray(_block_mask(H_A * DV_A, H_A * DK_A, DV_A, DK_A), F32))


def _prep_odd(i, wo_in, wo_conv, bo_conv, wo_rg, bo_rg, wo_ig, bo_ig, lam_lru, a_log, dt_bias, go_delta, wo_out,
              ln1_g, ln1_b, l):
    w_in = wo_in[i]
    win = jnp.concatenate([w_in, jnp.zeros((D_MODEL, O_END - w_in.shape[1]), w_in.dtype)], axis=1).astype(BF16)
    head_row = lambda v: jnp.pad(v.astype(F32), (H_D, LANE - 2 * H_D)).reshape(1, LANE)
    return dict(win=win, wcv=_pad_rows(wo_conv[i], 8), bcv=_row(bo_conv[i]),
                wg=jnp.concatenate([_block_diag(wo_rg[i]), _block_diag(wo_ig[i])], axis=1).astype(BF16),
                bg=_row(jnp.concatenate([bo_rg[i], bo_ig[i]])), lam=_row(lam_lru[i]),
                alog=head_row(a_log[i]), dtb=head_row(dt_bias[i]), gdl=_row(go_delta[i]),
                wout=wo_out[i].astype(BF16), lng=_row(ln1_g[l]), lnb=_row(ln1_b[l]),
                bd=jnp.asarray(_block_mask(H_D * CHUNK, H_D * CHUNK, CHUNK, CHUNK), BF16),
                bdk=jnp.asarray(_block_mask(H_D * CHUNK, H_D * DK_D, CHUNK, DK_D), BF16))


def _prep_ffn(l, w_up, w_fdw, b_fdw, w_down, ln2_g, ln2_b):
    return dict(wup=w_up[l].astype(BF16), wdw=_pad_rows(w_fdw[l], 8), bdw=_row(b_fdw[l]),
                wdown=w_down[l].astype(BF16), lng=_row(ln2_g[l]), lnb=_row(ln2_b[l]))


def _front_pad(buf, rows):
    return jnp.pad(buf, ((0, 0), (rows - buf.shape[1], 0), (0, 0)))


def _gla_state_to_blockdiag(s):
    s_t = jnp.swapaxes(s, 2, 3)
    return jnp.concatenate([jnp.pad(s_t[:, h], ((0, 0), (0, 0), (h * DK_A, (H_A - 1 - h) * DK_A)))
                            for h in range(H_A)], axis=1)


def _gla_state_from_blockdiag(s_bd):
    bsz = s_bd.shape[0]
    s5 = s_bd.reshape(bsz, H_A, DV_A, H_A, DK_A)
    return jnp.stack([jnp.swapaxes(s5[:, h, :, h, :], 1, 2) for h in range(H_A)], axis=1)


def _trunk(x, states, mix_w, ffn_w):
    L = x.shape[1]
    new_states = []
    for l in range(DEPTH):
        st = states[l]
        if l % 2 == 0:
            x, s_bd, buf = _even_layer(x, _gla_state_to_blockdiag(st[0]), _front_pad(st[1], 32), mix_w[l],
                                       min(L, EVEN_TILE))
            mix_new = (_gla_state_from_blockdiag(s_bd), buf[:, 32 - (W_B - 1):])
        else:
            x, h, s, buf = _odd_layer(x, st[0][:, None, :], st[1], _front_pad(st[2], 8), mix_w[l], min(L, ODD_TILE))
            mix_new = (h[:, 0], s, buf[:, 8 - (W_S - 1):])
        x, fbuf = _ffn_layer(x, _front_pad(st[-1], 8), ffn_w[l], min(L, FFN_TILE))
        new_states.append((*mix_new, fbuf[:, 8 - (W_F - 1):]))
    return x, new_states


def _zero_states(bsz):
    z = lambda *s: jnp.zeros((bsz,) + s, F32)
    return [(z(H_A, DK_A, DV_A), z(W_B - 1, D_B), z(W_F - 1, D_FF)) if l % 2 == 0 else
            (z(D_C), z(H_D, DK_D, DV_D), z(W_S - 1, CONV_ODD), z(W_F - 1, D_FF)) for l in range(DEPTH)]


def kernel(x_prompt, x_sample, state_l0_gla, cache_l0_dwconv, cache_l0_ffn, state_l1_lru, state_l1_delta, cache_l1_conv, cache_l1_ffn, state_l2_gla, cache_l2_dwconv, cache_l2_ffn, state_l3_lru, state_l3_delta, cache_l3_conv, cache_l3_ffn, we_in, we_lr, be_lr, ge_gla, we_dw, be_dw, ge_cn, be_cn, we_out, wo_in, wo_conv, bo_conv, wo_rg, bo_rg, wo_ig, bo_ig, lam_lru, a_log, dt_bias, go_delta, wo_out, w_up, w_fdw, b_fdw, w_down, ln1_g, ln1_b, ln2_g, ln2_b):
    mix_w = []
    for l in range(DEPTH):
        if l % 2 == 0:
            mix_w.append(_prep_even(l // 2, we_in, we_lr, be_lr, ge_gla, we_dw, be_dw, ge_cn, be_cn, we_out,
                                    ln1_g, ln1_b, l))
        else:
            mix_w.append(_prep_odd(l // 2, wo_in, wo_conv, bo_conv, wo_rg, bo_rg, wo_ig, bo_ig, lam_lru, a_log,
                                   dt_bias, go_delta, wo_out, ln1_g, ln1_b, l))
    ffn_w = [_prep_ffn(l, w_up, w_fdw, b_fdw, w_down, ln2_g, ln2_b) for l in range(DEPTH)]
    y_prompt, new_p = _trunk(x_prompt, _zero_states(x_prompt.shape[0]), mix_w, ffn_w)
    sample_states = [(state_l0_gla, cache_l0_dwconv, cache_l0_ffn),
                     (state_l1_lru, state_l1_delta, cache_l1_conv, cache_l1_ffn),
                     (state_l2_gla, cache_l2_dwconv, cache_l2_ffn),
                     (state_l3_lru, state_l3_delta, cache_l3_conv, cache_l3_ffn)]
    y_sample, new_s = _trunk(x_sample, sample_states, mix_w, ffn_w)
    flat = lambda ns: [a for layer in ns for a in layer]
    return (y_prompt, y_sample, *flat(new_p), *flat(new_s))
```

```python
import functools

import numpy as np
import jax
import jax.numpy as jnp
from jax import lax
from jax.experimental import pallas as pl
from jax.experimental.pallas import tpu as pltpu

F32 = jnp.float32
BF16 = jnp.bfloat16

D_MODEL = 1024
DEPTH = 4
CHUNK = 64
H_A, DK_A, DV_A, R_A, TAU_A = 4, 64, 128, 16, 16.0
D_B, W_B = 512, 31
D_C, H_C, DH_C, LRU_C = 512, 8, 64, 8.0
H_D, DK_D, DV_D, W_S = 4, 128, 128, 4
D_FF, W_F = 2688, 3
ALPHA = (2 * DEPTH) ** 0.25
EPS = 1e-5
CONV_ODD = D_C + 2 * H_D * DK_D + H_D * DV_D

LANE = 128
EVEN_TILE = 512
ODD_TILE = 512
FFN_TILE = 512
SUB_TILE = 256
VMEM_LIMIT = 56 * 1024 * 1024
MXU_TILE = 256
FFN_COLS = 5 * MXU_TILE
GLA_LEVELS = 6

E_Q, E_K, E_V, E_G, E_GA, E_GB, E_LR, E_END = 0, 256, 512, 1024, 1536, 2048, 2560, 2688
O_CONV, O_GC, O_Z, O_BA, O_END = 0, 2048, 2560, 3072, 3200


def _dot(a, b):
    return jnp.dot(a, b, preferred_element_type=F32)


def _dot_nt(a, b):
    return lax.dot_general(a, b, (((1,), (1,)), ((), ())), preferred_element_type=F32)


def _dot_tn(a, b):
    return lax.dot_general(a, b, (((0,), (0,)), ((), ())), preferred_element_type=F32)


def _split_dot(m, x):
    hi = x.astype(BF16)
    lo = (x - hi.astype(F32)).astype(BF16)
    return _dot(m, hi) + _dot(m, lo)


def _sigmoid(x):
    return jax.nn.sigmoid(x)


def _silu(x):
    return x * jax.nn.sigmoid(x)


def _softplus(x):
    return jnp.maximum(x, 0.0) + jnp.log(1.0 + jnp.exp(-jnp.abs(x)))


def _layernorm(x, g, b):
    mu = jnp.mean(x, axis=-1, keepdims=True)
    xc = x - mu
    var = jnp.mean(xc * xc, axis=-1, keepdims=True)
    return xc * lax.rsqrt(var + EPS) * g + b


def _rmsnorm(x):
    return x * lax.rsqrt(jnp.mean(x * x, axis=-1, keepdims=True) + EPS)


def _block_mask(rows, cols, rblk, cblk):
    r = np.arange(rows)[:, None] // rblk
    c = np.arange(cols)[None, :] // cblk
    return (r == c).astype(np.float32)


def _gla_constants(tm):
    c = CHUNK
    r = np.arange(c)
    tri = (r[None, :] <= r[:, None]).astype(np.float32)
    rest = (r[None, :] > r[:, None]).astype(np.float32)
    mats = [tri, rest]
    masks = [np.eye(c, dtype=np.float32)]
    for lvl in range(GLA_LEVELS):
        m = (c // 2) >> lvl
        anchor = (r // (2 * m)) * (2 * m) + m - 1
        mats.append(tri - tri[anchor])
        same = (r[:, None] // (2 * m)) == (r[None, :] // (2 * m))
        masks.append((same & ((r[:, None] % (2 * m)) >= m) & ((r[None, :] % (2 * m)) < m)).astype(np.float32))
    nc = tm // c
    call = np.concatenate([np.kron(np.eye(nc, dtype=np.float32), m) for m in mats], 0)
    return jnp.asarray(call, BF16), jnp.asarray(np.tile(np.stack(masks, 0), (1, nc, H_A)), F32)


def _spread(major, minor):
    out, j = [], 0
    for i, f in enumerate(major):
        out.append(f)
        while j < (i + 1) * len(minor) // len(major):
            out.append(minor[j])
            j += 1
    return out


def _emit(fns):
    for f in fns:
        f()


def _even_kernel_pipelined(x_ref, st0_ref, buf0_ref, win_ref, wlr_ref, blr_ref, ggla_ref, wdw_ref, bdw_ref, gcn_ref,
                           bcn_ref, wout_ref, lng_ref, lnb_ref, call_ref, masks_ref, bdk_ref, bdv_ref, bdst_ref,
                           y_ref, stnew_ref, bufnew_ref,
                           proj_s, shift_s, state_s, cat_s, *, tm, sub):
    t = pl.program_id(1)
    nt = pl.num_programs(1)

    @pl.when(t == 0)
    def _():
        state_s[...] = st0_ref[0]
        shift_s[0, 0:32, :] = buf0_ref[0]

    nc = sub // CHUNK
    chunks = [slice(c * CHUNK, (c + 1) * CHUNK) for c in range(nc)]
    carry = dict(st=state_s[...])
    vals = [dict() for _ in range(tm // sub)]

    def per_chunk(fn):
        return jnp.concatenate([fn(cs) for cs in chunks], axis=0)

    def proj(i):
        v, r0 = vals[i], i * sub
        def first():
            v['x'] = x_ref[0, r0:r0 + sub, :]
            v['xb'] = v['x'].astype(BF16)
        def cols(c0, c1):
            def f():
                proj_s[r0:r0 + sub, c0:c1] = _dot(v['xb'], win_ref[:, c0:c1])
            return f
        return [first] + [cols(c0, c1) for c0, c1 in ((E_GA, E_GB), (E_GB, E_LR), (E_Q, E_V), (E_V, E_G), (E_G, E_GA),
                                                      (E_LR, E_END))]

    def glu(i):
        r0 = i * sub
        def f():
            rows = slice(r0, r0 + sub)
            shift_s[0, 32 + r0:32 + r0 + sub, :] = proj_s[rows, E_GA:E_GB] * _sigmoid(proj_s[rows, E_GB:E_LR])
            lo = 0 if i == 0 else r0 + 24
            for s in range(1, 8):
                shift_s[s, lo:r0 + sub + 24, :] = shift_s[0, lo + s:r0 + sub + 24 + s, :]
        return [f]

    def conv(i):
        r0, rb = i * sub, 32
        def unit(rr):
            def f():
                acc = jnp.broadcast_to(bdw_ref[...], (rb, D_B))
                for j in range(W_B):
                    off = rr + 2 + j
                    acc = acc + wdw_ref[j:j + 1, :] * shift_s[off % 8, off - off % 8:off - off % 8 + rb, :]
                ob = _silu(_layernorm(acc, gcn_ref[...], bcn_ref[...]))
                cat_s[rr:rr + rb, D_B:2 * D_B] = ob.astype(BF16)
            return f
        return [unit(rr) for rr in range(r0, r0 + sub, rb)]

    def gla_prep(i):
        v, r0 = vals[i], i * sub
        rows = slice(r0, r0 + sub)
        def f():
            v['q'] = proj_s[rows, E_Q:E_K] * (DK_A ** -0.5)
            v['k'] = proj_s[rows, E_K:E_V]
            v['vb'] = proj_s[rows, E_V:E_G].astype(BF16)
            z = _dot(proj_s[rows, E_LR:E_END].astype(BF16), wlr_ref[...]) + blr_ref[...]
            la = -_softplus(-z) * (1.0 / TAU_A)
            v['e'] = _split_dot(call_ref[...], la)
        def g():
            bcum, brest = v['e'][0:sub], v['e'][sub:2 * sub]
            v['qe'] = (v['q'] * jnp.exp(bcum)).astype(BF16)
            kr = (v['k'] * jnp.exp(brest)).astype(BF16)
            v['dlast'] = jnp.exp(bcum + brest)
            v['kv'] = [bdst_ref[...] * _dot_tn(v['vb'][cs], kr[cs]) for cs in chunks]
            v['p'] = jnp.zeros((sub, H_A * CHUNK), F32)
        return [f, g]

    def gla_scores(i):
        v = vals[i]
        def level(lvl):
            def f():
                if lvl == 0:
                    qf, kf = v['q'].astype(BF16), v['k'].astype(BF16)
                else:
                    fac = jnp.exp(-jnp.abs(v['e'][(1 + lvl) * sub:(2 + lvl) * sub]))
                    qf, kf = (v['q'] * fac).astype(BF16), (v['k'] * fac).astype(BF16)
                v['p'] = v['p'] + masks_ref[lvl] * per_chunk(
                    lambda cs: _dot_nt(qf[cs], jnp.tile(kf[cs], (H_A, 1)) * bdk_ref[...]))
            return f
        return [level(lvl) for lvl in range(GLA_LEVELS + 1)]

    def gla_state(i):
        v = vals[i]
        def f():
            pb = v['p'].astype(BF16)
            o_intra = [_dot(pb[cs], jnp.tile(v['vb'][cs], (H_A, 1)) * bdv_ref[...]) for cs in chunks]
            st, sts = carry['st'], []
            for c in range(nc):
                sts.append(st.astype(BF16))
                st = st * v['dlast'][c * CHUNK:c * CHUNK + 1, :] + v['kv'][c]
            carry['st'] = st
            v['o'] = jnp.concatenate([_dot_nt(v['qe'][cs], sts[c]) + o_intra[c] for c, cs in enumerate(chunks)],
                                     axis=0)
        return [f]

    def output(i):
        v, r0 = vals[i], i * sub
        rows = slice(r0, r0 + sub)
        def head(h):
            def f():
                hs = slice(h * DV_A, (h + 1) * DV_A)
                oa = _rmsnorm(v['o'][:, hs]) * ggla_ref[:, hs] * _silu(proj_s[rows, E_G + h * DV_A:E_G + (h + 1) * DV_A])
                cat_s[rows, hs] = oa.astype(BF16)
            return f
        def final():
            y = _dot(cat_s[rows, :], wout_ref[...])
            y_ref[0, rows, :] = _layernorm(ALPHA * v['x'] + y, lng_ref[...], lnb_ref[...])
        return [head(h) for h in range(H_A)] + [final]

    if tm // sub == 1:
        for stage in (proj, glu, conv, gla_prep, gla_scores, gla_state, output):
            _emit(stage(0))
    else:
        _emit(proj(0)[:3])
        _emit(glu(0))
        _emit(_spread(conv(0), proj(0)[3:] + proj(1)[:3]))
        _emit(glu(1))
        _emit(_spread(conv(1), proj(1)[3:] + gla_prep(0)))
        _emit(gla_scores(0))
        _emit(gla_state(0))
        _emit(_spread(gla_prep(1) + gla_scores(1), output(0)))
        _emit(gla_state(1))
        _emit(output(1))
    shift_s[0, 0:32, :] = shift_s[0, tm:tm + 32, :]
    state_s[...] = carry['st']

    @pl.when(t == nt - 1)
    def _():
        stnew_ref[0] = state_s[...]
        bufnew_ref[0] = shift_s[0, 0:32, :]


def _even_layer(x, st_t, buf32, w, tm):
    bsz, L, _ = x.shape
    nt = L // tm
    full = lambda a: pl.BlockSpec(a.shape, lambda b, t: (0,) * a.ndim)
    weights = (w['win'], w['wlr'], w['blr'], w['ggla'], w['wdw'], w['bdw'], w['gcn'], w['bcn'], w['wout'],
               w['lng'], w['lnb'], *_gla_constants(min(tm, SUB_TILE)), w['bdk'], w['bdv'], w['bdst'])
    return pl.pallas_call(
        functools.partial(_even_kernel_pipelined, tm=tm, sub=min(tm, SUB_TILE)),
        grid=(bsz, nt),
        in_specs=[pl.BlockSpec((1, tm, D_MODEL), lambda b, t: (b, t, 0)),
                  pl.BlockSpec((1, H_A * DV_A, H_A * DK_A), lambda b, t: (b, 0, 0)),
                  pl.BlockSpec((1, 32, D_B), lambda b, t: (b, 0, 0))] + [full(a) for a in weights],
        out_specs=[pl.BlockSpec((1, tm, D_MODEL), lambda b, t: (b, t, 0)),
                   pl.BlockSpec((1, H_A * DV_A, H_A * DK_A), lambda b, t: (b, 0, 0)),
                   pl.BlockSpec((1, 32, D_B), lambda b, t: (b, 0, 0))],
        out_shape=[jax.ShapeDtypeStruct((bsz, L, D_MODEL), F32),
                   jax.ShapeDtypeStruct((bsz, H_A * DV_A, H_A * DK_A), F32),
                   jax.ShapeDtypeStruct((bsz, 32, D_B), F32)],
        scratch_shapes=[pltpu.VMEM((tm, E_END), F32),
                        pltpu.VMEM((8, tm + 32, D_B), F32),
                        pltpu.VMEM((H_A * DV_A, H_A * DK_A), F32),
                        pltpu.VMEM((tm, D_MODEL), BF16)],
        compiler_params=pltpu.CompilerParams(dimension_semantics=("arbitrary", "arbitrary"),
                                             vmem_limit_bytes=VMEM_LIMIT),
        name="even_mixer",
    )(x, st_t, buf32, *weights)


def _odd_kernel_pipelined(x_ref, h0_ref, st0_ref, buf0_ref, win_ref, wcv_ref, bcv_ref, wg_ref, bg_ref, lam_ref,
                          alog_ref, dtb_ref, gdl_ref, wout_ref, lng_ref, lnb_ref, bd_ref, bdk_ref, tri_ref, rest_ref,
                          y_ref, hnew_ref, stnew_ref, bufnew_ref,
                          cbuf_s, rest_s, cv_s, a_s, bx_s, hrow_s, state_s, cat_s, *, tm, sub):
    t = pl.program_id(1)
    nt = pl.num_programs(1)

    @pl.when(t == 0)
    def _():
        hrow_s[...] = h0_ref[0]
        state_s[...] = st0_ref[0]
        cbuf_s[0:8, :] = buf0_ref[0]

    nc = sub // CHUNK
    chunks = [slice(c * CHUNK, (c + 1) * CHUNK) for c in range(nc)]
    heads = [slice(h * CHUNK, (h + 1) * CHUNK) for h in range(H_D)]
    row = lax.broadcasted_iota(jnp.int32, (sub, H_D * CHUNK), 0) % CHUNK
    col = lax.broadcasted_iota(jnp.int32, (sub, H_D * CHUNK), 1) % CHUNK
    causal = (row >= col).astype(F32)
    strict = (row > col).astype(F32)
    eye = (row == col).astype(F32)
    sublane = lax.broadcasted_iota(jnp.int32, (sub, D_C), 0) % 8
    qoff, koff, voff = D_C, D_C + H_D * DK_D, D_C + 2 * H_D * DK_D
    carry = dict(h=jnp.broadcast_to(hrow_s[...], (8, D_C)), st=[state_s[h] for h in range(H_D)])
    vals = [dict() for _ in range(tm // sub)]

    def per_chunk(fn):
        return jnp.concatenate([fn(cs) for cs in chunks], axis=0)

    def blockdiag(m):
        return jnp.tile(m, (H_D, 1)) * bd_ref[...]

    def proj(i):
        v, r0 = vals[i], i * sub
        def first():
            v['x'] = x_ref[0, r0:r0 + sub, :]
            v['xb'] = v['x'].astype(BF16)
        def conv_cols(c0):
            def f():
                cbuf_s[8 + r0:8 + r0 + sub, c0:c0 + D_C] = _dot(v['xb'], win_ref[:, c0:c0 + D_C])
            return f
        def rest_cols(c0, c1):
            def f():
                rest_s[r0:r0 + sub, c0 - O_GC:c1 - O_GC] = _dot(v['xb'], win_ref[:, c0:c1])
            return f
        return [first] + [conv_cols(c0) for c0 in range(O_CONV, O_GC, D_C)] + [
            rest_cols(O_GC, O_Z), rest_cols(O_Z, O_END)]

    def conv(i):
        r0, rb = i * sub, 32
        def unit(rr, cb):
            def f():
                cs = slice(cb * D_C, (cb + 1) * D_C)
                acc = jnp.broadcast_to(bcv_ref[:, cs], (rb, D_C))
                for j in range(W_S):
                    acc = acc + wcv_ref[j:j + 1, cs] * cbuf_s[rr + 5 + j:rr + 5 + j + rb, cs]
                cv_s[rr:rr + rb, cs] = acc if cb == 0 else _silu(acc)
            return f
        return [unit(rr, cb) for rr in range(r0, r0 + sub, rb) for cb in range(CONV_ODD // D_C)]

    def lru_prep(i):
        r0 = i * sub
        def f():
            xc = cv_s[r0:r0 + sub, 0:D_C]
            gates = _dot(xc.astype(BF16), wg_ref[...]) + bg_ref[...]
            log_a = LRU_C * _sigmoid(gates[:, 0:D_C]) * (-_softplus(-lam_ref[...]))
            av = jnp.exp(log_a)
            one_m_a2 = -jnp.tanh(log_a) * (jnp.exp(2.0 * log_a) + 1.0)
            bv = jnp.sqrt(one_m_a2) * (_sigmoid(gates[:, D_C:2 * D_C]) * xc)
            for d in (1, 2, 4):
                keep = sublane >= d
                a_up = jnp.where(keep, pltpu.roll(av, d, 0), 1.0)
                b_up = jnp.where(keep, pltpu.roll(bv, d, 0), 0.0)
                bv = av * b_up + bv
                av = av * a_up
            a_s[r0:r0 + sub, :] = av
            bx_s[r0:r0 + sub, :] = bv
        return [f]

    def lru_scan(i):
        r0 = i * sub
        def group(rr):
            def f():
                hg = a_s[rr:rr + 8, :] * carry['h'] + bx_s[rr:rr + 8, :]
                bx_s[rr:rr + 8, :] = hg
                carry['h'] = jnp.broadcast_to(hg[7:8, :], (8, D_C))
            return f
        def finish():
            cat_s[r0:r0 + sub, 0:D_C] = (bx_s[r0:r0 + sub, :] * jax.nn.gelu(rest_s[r0:r0 + sub, 0:D_C])).astype(BF16)
        return [group(rr) for rr in range(r0, r0 + sub, 8)] + [finish]

    def delta_prep(i):
        v, r0 = vals[i], i * sub
        rows = slice(r0, r0 + sub)
        def gates_f():
            ba = rest_s[rows, O_BA - O_GC:O_END - O_GC]
            v['beta'] = _sigmoid(ba)
            g_all = -jnp.exp(alog_ref[...]) * _softplus(ba + dtb_ref[...])
            v['gcum'] = _split_dot(tri_ref[...], g_all)
            v['grest'] = _split_dot(rest_ref[...], g_all)
            gb = jnp.concatenate([jnp.broadcast_to(g_all[:, H_D + h:H_D + h + 1], (sub, CHUNK)) for h in range(H_D)],
                                 axis=1)
            v['edm'] = jnp.exp(_split_dot(tri_ref[...], gb * strict))
            for key in ('kn', 'kb', 'qn', 'rhs', 'qg', 'kg', 'eg'):
                v[key] = []
        def head(h):
            def f():
                qh = cv_s[rows, qoff + h * DK_D:qoff + (h + 1) * DK_D]
                kh = cv_s[rows, koff + h * DK_D:koff + (h + 1) * DK_D]
                vh = cv_s[rows, voff + h * DV_D:voff + (h + 1) * DV_D]
                qn = qh * lax.rsqrt(jnp.sum(qh * qh, axis=-1, keepdims=True) + 1e-6) * (DK_D ** -0.5)
                kn = kh * lax.rsqrt(jnp.sum(kh * kh, axis=-1, keepdims=True) + 1e-6)
                beta = v['beta'][:, h:h + 1]
                gc = v['gcum'][:, H_D + h:H_D + h + 1]
                gr = v['grest'][:, H_D + h:H_D + h + 1]
                egc = jnp.exp(gc)
                kb = kn * beta
                v['kn'].append(kn.astype(BF16))
                v['kb'].append(kb.astype(BF16))
                v['qn'].append(qn.astype(BF16))
                v['rhs'].append(jnp.concatenate([vh * beta, kb * egc], axis=-1).astype(BF16))
                v['qg'].append((qn * egc).astype(BF16))
                v['kg'].append((kn * jnp.exp(gr)).astype(BF16))
                v['eg'].append(jnp.exp(gc + gr))
            return f
        return [gates_f] + [head(h) for h in range(H_D)]

    def delta_scores(i):
        v = vals[i]
        def f():
            kb_all = jnp.concatenate(v['kb'], axis=1)
            qn_all = jnp.concatenate(v['qn'], axis=1)
            kn_all = jnp.concatenate(v['kn'], axis=1)
            aq = [_dot_nt(jnp.concatenate([kb_all[cs], qn_all[cs]], axis=0),
                          jnp.tile(kn_all[cs], (H_D, 1)) * bdk_ref[...]) for cs in chunks]
            n = -(jnp.concatenate([a[0:CHUNK] for a in aq], axis=0) * v['edm'] * strict)
            v['attn'] = (jnp.concatenate([a[CHUNK:2 * CHUNK] for a in aq], axis=0) * v['edm'] * causal).astype(BF16)
            v['tt'] = eye + n
            v['pw'] = n
        return [f]

    def delta_inverse(i):
        v = vals[i]
        def square():
            pwb = v['pw'].astype(BF16)
            v['pw'] = per_chunk(lambda cs: _dot(pwb[cs], blockdiag(pwb[cs])))
        def extend():
            pwb, ttb = v['pw'].astype(BF16), v['tt'].astype(BF16)
            v['tt'] = v['tt'] + per_chunk(lambda cs: _dot(pwb[cs], blockdiag(ttb[cs])))
        return [square, extend] * 5

    def delta_state(i):
        v, r0 = vals[i], i * sub
        def solve():
            ttb = v['tt'].astype(BF16)
            v['sol'] = [[_dot(ttb[cs, hs], v['rhs'][h][cs]) for h, hs in enumerate(heads)] for cs in chunks]
            v['o'] = [[] for _ in range(H_D)]
        def step(c):
            def f():
                cs, st, sol = chunks[c], carry['st'], v['sol'][c]
                stb = [s.astype(BF16) for s in st]
                ws = [_dot(jnp.concatenate([sol[h][:, DV_D:].astype(BF16), v['qg'][h][cs]], axis=0), stb[h])
                      for h in range(H_D)]
                vnb = [(sol[h][:, 0:DV_D] - ws[h][0:CHUNK]).astype(BF16) for h in range(H_D)]
                for h in range(H_D):
                    v['o'][h].append(ws[h][CHUNK:2 * CHUNK] + _dot(v['attn'][cs, heads[h]], vnb[h]))
                carry['st'] = [v['eg'][h][c * CHUNK:c * CHUNK + 1, :] * st[h] + _dot_tn(v['kg'][h][cs], vnb[h])
                               for h in range(H_D)]
            return f
        return [solve] + [step(c) for c in range(nc)]

    def output(i):
        v, r0 = vals[i], i * sub
        rows = slice(r0, r0 + sub)
        def head(h):
            def f():
                o = jnp.concatenate(v['o'][h], axis=0)
                zg = rest_s[rows, O_Z - O_GC + h * DV_D:O_Z - O_GC + (h + 1) * DV_D]
                od = _rmsnorm(o) * gdl_ref[:, h * DV_D:(h + 1) * DV_D] * _silu(zg)
                cat_s[rows, D_C + h * DV_D:D_C + (h + 1) * DV_D] = od.astype(BF16)
            return f
        def final():
            y = _dot(cat_s[rows, :], wout_ref[...])
            y_ref[0, rows, :] = _layernorm(ALPHA * v['x'] + y, lng_ref[...], lnb_ref[...])
        return [head(h) for h in range(H_D)] + [final]

    if tm // sub == 1:
        for stage in (proj, conv, lru_prep):
            _emit(stage(0))
        _emit(_spread(lru_scan(0), delta_prep(0)))
        for stage in (delta_scores, delta_inverse, delta_state, output):
            _emit(stage(0))
    else:
        _emit(proj(0))
        _emit(_spread(conv(0), proj(1)))
        _emit(lru_prep(0))
        _emit(_spread(lru_scan(0), delta_prep(0)))
        _emit(delta_scores(0))
        _emit(_spread(conv(1), delta_inverse(0)))
        _emit(lru_prep(1))
        _emit(_spread(_spread(lru_scan(1), delta_prep(1)), delta_state(0)))
        _emit(delta_scores(1))
        _emit(_spread(delta_inverse(1), output(0)))
        _emit(delta_state(1))
        _emit(output(1))
    cbuf_s[0:8, :] = cbuf_s[tm:tm + 8, :]
    hrow_s[...] = carry['h'][0:1, :]
    for h in range(H_D):
        state_s[h] = carry['st'][h]

    @pl.when(t == nt - 1)
    def _():
        hnew_ref[0] = hrow_s[...]
        stnew_ref[0] = state_s[...]
        bufnew_ref[0] = cbuf_s[0:8, :]


def _chunk_sum_matrices(tm):
    r = np.arange(tm)
    same = (r[:, None] // CHUNK) == (r[None, :] // CHUNK)
    tri = same & (r[None, :] <= r[:, None])
    rest = same & (r[None, :] > r[:, None])
    return jnp.asarray(tri.astype(np.float32), BF16), jnp.asarray(rest.astype(np.float32), BF16)


def _odd_layer(x, h0, st0, buf8, w, tm):
    bsz, L, _ = x.shape
    nt = L // tm
    full = lambda a: pl.BlockSpec(a.shape, lambda b, t: (0,) * a.ndim)
    weights = (w['win'], w['wcv'], w['bcv'], w['wg'], w['bg'], w['lam'], w['alog'], w['dtb'], w['gdl'], w['wout'],
               w['lng'], w['lnb'], w['bd'], w['bdk'], *_chunk_sum_matrices(min(tm, SUB_TILE)))
    return pl.pallas_call(
        functools.partial(_odd_kernel_pipelined, tm=tm, sub=min(tm, SUB_TILE)),
        grid=(bsz, nt),
        in_specs=[pl.BlockSpec((1, tm, D_MODEL), lambda b, t: (b, t, 0)),
                  pl.BlockSpec((1, 1, D_C), lambda b, t: (b, 0, 0)),
                  pl.BlockSpec((1, H_D, DK_D, DV_D), lambda b, t: (b, 0, 0, 0)),
                  pl.BlockSpec((1, 8, CONV_ODD), lambda b, t: (b, 0, 0))] + [full(a) for a in weights],
        out_specs=[pl.BlockSpec((1, tm, D_MODEL), lambda b, t: (b, t, 0)),
                   pl.BlockSpec((1, 1, D_C), lambda b, t: (b, 0, 0)),
                   pl.BlockSpec((1, H_D, DK_D, DV_D), lambda b, t: (b, 0, 0, 0)),
                   pl.BlockSpec((1, 8, CONV_ODD), lambda b, t: (b, 0, 0))],
        out_shape=[jax.ShapeDtypeStruct((bsz, L, D_MODEL), F32),
                   jax.ShapeDtypeStruct((bsz, 1, D_C), F32),
                   jax.ShapeDtypeStruct((bsz, H_D, DK_D, DV_D), F32),
                   jax.ShapeDtypeStruct((bsz, 8, CONV_ODD), F32)],
        scratch_shapes=[pltpu.VMEM((tm + 8, CONV_ODD), F32),
                        pltpu.VMEM((tm, O_END - O_GC), F32),
                        pltpu.VMEM((tm, CONV_ODD), F32),
                        pltpu.VMEM((tm, D_C), F32),
                        pltpu.VMEM((tm, D_C), F32),
                        pltpu.VMEM((1, D_C), F32),
                        pltpu.VMEM((H_D, DK_D, DV_D), F32),
                        pltpu.VMEM((tm, D_MODEL), BF16)],
        compiler_params=pltpu.CompilerParams(dimension_semantics=("arbitrary", "arbitrary"),
                                             vmem_limit_bytes=VMEM_LIMIT),
        name="odd_mixer",
    )(x, h0, st0, buf8, *weights)


def _ffn_kernel(x_ref, buf0_ref, wup_ref, wdw_ref, bdw_ref, wdown_ref, lng_ref, lnb_ref,
                y_ref, bufnew_ref, hbuf_s, *, tm):
    t = pl.program_id(1)
    nt = pl.num_programs(1)

    @pl.when(t == 0)
    def _():
        hbuf_s[0:8, :] = buf0_ref[0]

    x = x_ref[0]
    xb = x.astype(BF16)
    acc = jnp.zeros((tm, D_MODEL), F32)
    for c0 in range(0, D_FF, FFN_COLS):
        cs = slice(c0, min(c0 + FFN_COLS, D_FF))
        hg = _dot(xb, wup_ref[:, cs])
        hbuf_s[8:8 + tm, cs] = hg
        cv = (bdw_ref[:, cs] + wdw_ref[0:1, cs] * hbuf_s[6:6 + tm, cs] + wdw_ref[1:2, cs] * hbuf_s[7:7 + tm, cs]
              + wdw_ref[2:3, cs] * hg)
        hv = _dot(xb, wup_ref[:, D_FF + cs.start:D_FF + cs.stop])
        act = jax.nn.gelu(cv) * hv
        acc = acc + _dot(act.astype(BF16), wdown_ref[cs, :])
    hbuf_s[0:8, :] = hbuf_s[tm:tm + 8, :]
    y_ref[0] = _layernorm(ALPHA * x + acc, lng_ref[...], lnb_ref[...])

    @pl.when(t == nt - 1)
    def _():
        bufnew_ref[0] = hbuf_s[0:8, :]


def _ffn_layer(x, buf8, w, tm):
    bsz, L, _ = x.shape
    nt = L // tm
    full = lambda a: pl.BlockSpec(a.shape, lambda b, t: (0,) * a.ndim)
    weights = (w['wup'], w['wdw'], w['bdw'], w['wdown'], w['lng'], w['lnb'])
    return pl.pallas_call(
        functools.partial(_ffn_kernel, tm=tm),
        grid=(bsz, nt),
        in_specs=[pl.BlockSpec((1, tm, D_MODEL), lambda b, t: (b, t, 0)),
                  pl.BlockSpec((1, 8, D_FF), lambda b, t: (b, 0, 0))] + [full(a) for a in weights],
        out_specs=[pl.BlockSpec((1, tm, D_MODEL), lambda b, t: (b, t, 0)),
                   pl.BlockSpec((1, 8, D_FF), lambda b, t: (b, 0, 0))],
        out_shape=[jax.ShapeDtypeStruct((bsz, L, D_MODEL), F32),
                   jax.ShapeDtypeStruct((bsz, 8, D_FF), F32)],
        scratch_shapes=[pltpu.VMEM((tm + 8, D_FF), F32)],
        compiler_params=pltpu.CompilerParams(dimension_semantics=("arbitrary", "arbitrary"),
                                             vmem_limit_bytes=VMEM_LIMIT),
        name="conv_ffn",
    )(x, buf8, *weights)


def _row(v):
    return v.reshape(1, -1).astype(F32)


def _pad_rows(a, rows):
    return jnp.pad(a, ((0, rows - a.shape[0]), (0, 0)))


def _block_diag(w):
    h, d, _ = w.shape
    return jnp.einsum('hij,hg->higj', w, jnp.eye(h, dtype=w.dtype)).reshape(h * d, h * d)


def _prep_even(i, we_in, we_lr, be_lr, ge_gla, we_dw, be_dw, ge_cn, be_cn, we_out, ln1_g, ln1_b, l):
    w_in = we_in[i]
    lr0 = 2 * H_A * DK_A + 2 * H_A * DV_A
    win = jnp.concatenate([w_in[:, :lr0], w_in[:, lr0 + R_A:], w_in[:, lr0:lr0 + R_A],
                           jnp.zeros((D_MODEL, LANE - R_A), w_in.dtype)], axis=1).astype(BF16)
    return dict(win=win, wlr=_pad_rows(we_lr[i], LANE).astype(BF16), blr=_row(be_lr[i]), ggla=_row(ge_gla[i]),
                wdw=_pad_rows(we_dw[i], 32), bdw=_row(be_dw[i]), gcn=_row(ge_cn[i]), bcn=_row(be_cn[i]),
                wout=we_out[i].astype(BF16), lng=_row(ln1_g[l]), lnb=_row(ln1_b[l]),
                bdk=jnp.asarray(_block_mask(H_A * CHUNK, H_A * DK_A, CHUNK, DK_A), BF16),
                bdv=jnp.asarray(_block_mask(H_A * CHUNK, H_A * DV_A, CHUNK, DV_A), BF16),
                bdst=jnp.asarray(_block_mask(H_A * DV_A, H_A * DK_A, DV_A, DK_A), F32))


def _prep_odd(i, wo_in, wo_conv, bo_conv, wo_rg, bo_rg, wo_ig, bo_ig, lam_lru, a_log, dt_bias, go_delta, wo_out,
              ln1_g, ln1_b, l):
    w_in = wo_in[i]
    win = jnp.concatenate([w_in, jnp.zeros((D_MODEL, O_END - w_in.shape[1]), w_in.dtype)], axis=1).astype(BF16)
    head_row = lambda v: jnp.pad(v.astype(F32), (H_D, LANE - 2 * H_D)).reshape(1, LANE)
    return dict(win=win, wcv=_pad_rows(wo_conv[i], 8), bcv=_row(bo_conv[i]),
                wg=jnp.concatenate([_block_diag(wo_rg[i]), _block_diag(wo_ig[i])], axis=1).astype(BF16),
                bg=_row(jnp.concatenate([bo_rg[i], bo_ig[i]])), lam=_row(lam_lru[i]),
                alog=head_row(a_log[i]), dtb=head_row(dt_bias[i]), gdl=_row(go_delta[i]),
                wout=wo_out[i].astype(BF16), lng=_row(ln1_g[l]), lnb=_row(ln1_b[l]),
                bd=jnp.asarray(_block_mask(H_D * CHUNK, H_D * CHUNK, CHUNK, CHUNK), BF16),
                bdk=jnp.asarray(_block_mask(H_D * CHUNK, H_D * DK_D, CHUNK, DK_D), BF16))


def _prep_ffn(l, w_up, w_fdw, b_fdw, w_down, ln2_g, ln2_b):
    return dict(wup=w_up[l].astype(BF16), wdw=_pad_rows(w_fdw[l], 8), bdw=_row(b_fdw[l]),
                wdown=w_down[l].astype(BF16), lng=_row(ln2_g[l]), lnb=_row(ln2_b[l]))


def _front_pad(buf, rows):
    return jnp.pad(buf, ((0, 0), (rows - buf.shape[1], 0), (0, 0)))


def _gla_state_to_blockdiag(s):
    s_t = jnp.swapaxes(s, 2, 3)
    return jnp.concatenate([jnp.pad(s_t[:, h], ((0, 0), (0, 0), (h * DK_A, (H_A - 1 - h) * DK_A)))
                            for h in range(H_A)], axis=1)


def _gla_state_from_blockdiag(s_bd):
    bsz = s_bd.shape[0]
    s5 = s_bd.reshape(bsz, H_A, DV_A, H_A, DK_A)
    return jnp.stack([jnp.swapaxes(s5[:, h, :, h, :], 1, 2) for h in range(H_A)], axis=1)


def _trunk(x, states, mix_w, ffn_w):
    L = x.shape[1]
    new_states = []
    for l in range(DEPTH):
        st = states[l]
        if l % 2 == 0:
            x, s_bd, buf = _even_layer(x, _gla_state_to_blockdiag(st[0]), _front_pad(st[1], 32), mix_w[l],
                                       min(L, EVEN_TILE))
            mix_new = (_gla_state_from_blockdiag(s_bd), buf[:, 32 - (W_B - 1):])
        else:
            x, h, s, buf = _odd_layer(x, st[0][:, None, :], st[1], _front_pad(st[2], 8), mix_w[l], min(L, ODD_TILE))
            mix_new = (h[:, 0], s, buf[:, 8 - (W_S - 1):])
        x, fbuf = _ffn_layer(x, _front_pad(st[-1], 8), ffn_w[l], min(L, FFN_TILE))
        new_states.append((*mix_new, fbuf[:, 8 - (W_F - 1):]))
    return x, new_states


def _zero_states(bsz):
    z = lambda *s: jnp.zeros((bsz,) + s, F32)
    return [(z(H_A, DK_A, DV_A), z(W_B - 1, D_B), z(W_F - 1, D_FF)) if l % 2 == 0 else
            (z(D_C), z(H_D, DK_D, DV_D), z(W_S - 1, CONV_ODD), z(W_F - 1, D_FF)) for l in range(DEPTH)]


def kernel(x_prompt, x_sample, state_l0_gla, cache_l0_dwconv, cache_l0_ffn, state_l1_lru, state_l1_delta, cache_l1_conv, cache_l1_ffn, state_l2_gla, cache_l2_dwconv, cache_l2_ffn, state_l3_lru, state_l3_delta, cache_l3_conv, cache_l3_ffn, we_in, we_lr, be_lr, ge_gla, we_dw, be_dw, ge_cn, be_cn, we_out, wo_in, wo_conv, bo_conv, wo_rg, bo_rg, wo_ig, bo_ig, lam_lru, a_log, dt_bias, go_delta, wo_out, w_up, w_fdw, b_fdw, w_down, ln1_g, ln1_b, ln2_g, ln2_b):
    mix_w = []
    for l in range(DEPTH):
        if l % 2 == 0:
            mix_w.append(_prep_even(l // 2, we_in, we_lr, be_lr, ge_gla, we_dw, be_dw, ge_cn, be_cn, we_out,
                                    ln1_g, ln1_b, l))
        else:
            mix_w.append(_prep_odd(l // 2, wo_in, wo_conv, bo_conv, wo_rg, bo_rg, wo_ig, bo_ig, lam_lru, a_log,
                                   dt_bias, go_delta, wo_out, ln1_g, ln1_b, l))
    ffn_w = [_prep_ffn(l, w_up, w_fdw, b_fdw, w_down, ln2_g, ln2_b) for l in range(DEPTH)]
    y_prompt, new_p = _trunk(x_prompt, _zero_states(x_prompt.shape[0]), mix_w, ffn_w)
    sample_states = [(state_l0_gla, cache_l0_dwconv, cache_l0_ffn),
                     (state_l1_lru, state_l1_delta, cache_l1_conv, cache_l1_ffn),
                     (state_l2_gla, cache_l2_dwconv, cache_l2_ffn),
                     (state_l3_lru, state_l3_delta, cache_l3_conv, cache_l3_ffn)]
    y_sample, new_s = _trunk(x_sample, sample_states, mix_w, ffn_w)
    flat = lambda ns: [a for layer in ns for a in layer]
    return (y_prompt, y_sample, *flat(new_p), *flat(new_s))
```

```python
import functools

import numpy as np
import jax
import jax.numpy as jnp
from jax import lax
from jax.experimental import pallas as pl
from jax.experimental.pallas import tpu as pltpu

F32 = jnp.float32
BF16 = jnp.bfloat16

D_MODEL = 1024
DEPTH = 4
CHUNK = 64
H_A, DK_A, DV_A, R_A, TAU_A = 4, 64, 128, 16, 16.0
D_B, W_B = 512, 31
D_C, H_C, DH_C, LRU_C = 512, 8, 64, 8.0
H_D, DK_D, DV_D, W_S = 4, 128, 128, 4
D_FF, W_F = 2688, 3
ALPHA = (2 * DEPTH) ** 0.25
EPS = 1e-5
CONV_ODD = D_C + 2 * H_D * DK_D + H_D * DV_D

LANE = 128
EVEN_TILE = 512
ODD_TILE = 512
FFN_TILE = 512
SUB_TILE = 256
VMEM_LIMIT = 56 * 1024 * 1024
MXU_TILE = 256
FFN_COLS = 5 * MXU_TILE
FFN_MAIN = D_FF // FFN_COLS * FFN_COLS
GLA_LEVELS = 6

E_Q, E_K, E_V, E_G, E_GA, E_GB, E_LR, E_END = 0, 256, 512, 1024, 1536, 2048, 2560, 2688
O_CONV, O_GC, O_Z, O_BA, O_END = 0, 2048, 2560, 3072, 3200


def _dot(a, b):
    return jnp.dot(a, b, preferred_element_type=F32)


def _dot_nt(a, b):
    return lax.dot_general(a, b, (((1,), (1,)), ((), ())), preferred_element_type=F32)


def _dot_tn(a, b):
    return lax.dot_general(a, b, (((0,), (0,)), ((), ())), preferred_element_type=F32)


def _split_dot(m, x):
    hi = x.astype(BF16)
    lo = (x - hi.astype(F32)).astype(BF16)
    return _dot(m, hi) + _dot(m, lo)


def _sigmoid(x):
    return jax.nn.sigmoid(x)


def _silu(x):
    return x * jax.nn.sigmoid(x)


def _softplus(x):
    return jnp.maximum(x, 0.0) + jnp.log(1.0 + jnp.exp(-jnp.abs(x)))


def _layernorm(x, g, b):
    mu = jnp.mean(x, axis=-1, keepdims=True)
    xc = x - mu
    var = jnp.mean(xc * xc, axis=-1, keepdims=True)
    return xc * lax.rsqrt(var + EPS) * g + b


def _rmsnorm(x):
    return x * lax.rsqrt(jnp.mean(x * x, axis=-1, keepdims=True) + EPS)


def _block_mask(rows, cols, rblk, cblk):
    r = np.arange(rows)[:, None] // rblk
    c = np.arange(cols)[None, :] // cblk
    return (r == c).astype(np.float32)


def _gla_constants(tm):
    c = CHUNK
    r = np.arange(c)
    tri = (r[None, :] <= r[:, None]).astype(np.float32)
    rest = (r[None, :] > r[:, None]).astype(np.float32)
    mats = [tri, rest]
    masks = [np.eye(c, dtype=np.float32)]
    for lvl in range(GLA_LEVELS):
        m = (c // 2) >> lvl
        anchor = (r // (2 * m)) * (2 * m) + m - 1
        mats.append(tri - tri[anchor])
        same = (r[:, None] // (2 * m)) == (r[None, :] // (2 * m))
        masks.append((same & ((r[:, None] % (2 * m)) >= m) & ((r[None, :] % (2 * m)) < m)).astype(np.float32))
    nc = tm // c
    call = np.concatenate([np.kron(np.eye(nc, dtype=np.float32), m) for m in mats], 0)
    return jnp.asarray(call, BF16), jnp.asarray(np.tile(np.stack(masks, 0), (1, nc, H_A)), F32)


def _spread(major, minor):
    out, j = [], 0
    for i, f in enumerate(major):
        out.append(f)
        while j < (i + 1) * len(minor) // len(major):
            out.append(minor[j])
            j += 1
    return out


def _emit(fns):
    for f in fns:
        f()


def _even_kernel_pipelined(x_ref, st0_ref, buf0_ref, win_ref, wlr_ref, blr_ref, ggla_ref, wdw_ref, bdw_ref, gcn_ref,
                           bcn_ref, wout_ref, lng_ref, lnb_ref, call_ref, masks_ref, bdk_ref, bdv_ref, bdst_ref,
                           y_ref, stnew_ref, bufnew_ref,
                           proj_s, shift_s, state_s, cat_s, *, tm, sub):
    t = pl.program_id(1)
    nt = pl.num_programs(1)

    @pl.when(t == 0)
    def _():
        state_s[...] = st0_ref[0]
        shift_s[0, 0:32, :] = buf0_ref[0]

    nc = sub // CHUNK
    chunks = [slice(c * CHUNK, (c + 1) * CHUNK) for c in range(nc)]
    carry = dict(st=state_s[...])
    vals = [dict() for _ in range(tm // sub)]

    def per_chunk(fn):
        return jnp.concatenate([fn(cs) for cs in chunks], axis=0)

    def proj(i):
        v, r0 = vals[i], i * sub
        def first():
            v['x'] = x_ref[0, r0:r0 + sub, :]
            v['xb'] = v['x'].astype(BF16)
        def cols(c0, c1):
            def f():
                proj_s[r0:r0 + sub, c0:c1] = _dot(v['xb'], win_ref[:, c0:c1])
            return f
        return [first] + [cols(c0, c1) for c0, c1 in ((E_GA, E_GB), (E_GB, E_LR), (E_Q, E_V), (E_V, E_G), (E_G, E_GA),
                                                      (E_LR, E_END))]

    def glu(i):
        r0 = i * sub
        def f():
            rows = slice(r0, r0 + sub)
            shift_s[0, 32 + r0:32 + r0 + sub, :] = proj_s[rows, E_GA:E_GB] * _sigmoid(proj_s[rows, E_GB:E_LR])
            lo = 0 if i == 0 else r0 + 24
            for s in range(1, 8):
                shift_s[s, lo:r0 + sub + 24, :] = shift_s[0, lo + s:r0 + sub + 24 + s, :]
        return [f]

    def conv(i):
        r0, rb = i * sub, 32
        def unit(rr):
            def f():
                acc = jnp.broadcast_to(bdw_ref[...], (rb, D_B))
                for j in range(W_B):
                    off = rr + 2 + j
                    acc = acc + wdw_ref[j:j + 1, :] * shift_s[off % 8, off - off % 8:off - off % 8 + rb, :]
                ob = _silu(_layernorm(acc, gcn_ref[...], bcn_ref[...]))
                cat_s[rr:rr + rb, D_B:2 * D_B] = ob.astype(BF16)
            return f
        return [unit(rr) for rr in range(r0, r0 + sub, rb)]

    def gla_prep(i):
        v, r0 = vals[i], i * sub
        rows = slice(r0, r0 + sub)
        def f():
            v['q'] = proj_s[rows, E_Q:E_K] * (DK_A ** -0.5)
            v['k'] = proj_s[rows, E_K:E_V]
            v['vb'] = proj_s[rows, E_V:E_G].astype(BF16)
            z = _dot(proj_s[rows, E_LR:E_END].astype(BF16), wlr_ref[...]) + blr_ref[...]
            la = -_softplus(-z) * (1.0 / TAU_A)
            v['e'] = _split_dot(call_ref[...], la)
        def g():
            bcum, brest = v['e'][0:sub], v['e'][sub:2 * sub]
            v['qe'] = (v['q'] * jnp.exp(bcum)).astype(BF16)
            kr = (v['k'] * jnp.exp(brest)).astype(BF16)
            v['dlast'] = jnp.exp(bcum + brest)
            v['kv'] = [bdst_ref[...] * _dot_tn(v['vb'][cs], kr[cs]) for cs in chunks]
            v['p'] = jnp.zeros((sub, H_A * CHUNK), F32)
        return [f, g]

    def gla_scores(i):
        v = vals[i]
        def level(lvl):
            def f():
                if lvl == 0:
                    qf, kf = v['q'].astype(BF16), v['k'].astype(BF16)
                else:
                    fac = jnp.exp(-jnp.abs(v['e'][(1 + lvl) * sub:(2 + lvl) * sub]))
                    qf, kf = (v['q'] * fac).astype(BF16), (v['k'] * fac).astype(BF16)
                v['p'] = v['p'] + masks_ref[lvl] * per_chunk(
                    lambda cs: _dot_nt(qf[cs], jnp.tile(kf[cs], (H_A, 1)) * bdk_ref[...]))
            return f
        return [level(lvl) for lvl in range(GLA_LEVELS + 1)]

    def gla_state(i):
        v = vals[i]
        def f():
            pb = v['p'].astype(BF16)
            o_intra = [_dot(pb[cs], jnp.tile(v['vb'][cs], (H_A, 1)) * bdv_ref[...]) for cs in chunks]
            st, sts = carry['st'], []
            for c in range(nc):
                sts.append(st.astype(BF16))
                st = st * v['dlast'][c * CHUNK:c * CHUNK + 1, :] + v['kv'][c]
            carry['st'] = st
            v['o'] = jnp.concatenate([_dot_nt(v['qe'][cs], sts[c]) + o_intra[c] for c, cs in enumerate(chunks)],
                                     axis=0)
        return [f]

    def output(i):
        v, r0 = vals[i], i * sub
        rows = slice(r0, r0 + sub)
        def head(h):
            def f():
                hs = slice(h * DV_A, (h + 1) * DV_A)
                oa = _rmsnorm(v['o'][:, hs]) * ggla_ref[:, hs] * _silu(proj_s[rows, E_G + h * DV_A:E_G + (h + 1) * DV_A])
                cat_s[rows, hs] = oa.astype(BF16)
            return f
        def final():
            y = _dot(cat_s[rows, :], wout_ref[...])
            y_ref[0, rows, :] = _layernorm(ALPHA * v['x'] + y, lng_ref[...], lnb_ref[...])
        return [head(h) for h in range(H_A)] + [final]

    if tm // sub == 1:
        for stage in (proj, glu, conv, gla_prep, gla_scores, gla_state, output):
            _emit(stage(0))
    else:
        _emit(proj(0)[:3])
        _emit(glu(0))
        _emit(_spread(conv(0), proj(0)[3:] + proj(1)[:3]))
        _emit(glu(1))
        _emit(_spread(conv(1), proj(1)[3:] + gla_prep(0)))
        _emit(gla_scores(0))
        _emit(gla_state(0))
        _emit(_spread(gla_prep(1) + gla_scores(1), output(0)))
        _emit(gla_state(1))
        _emit(output(1))
    shift_s[0, 0:32, :] = shift_s[0, tm:tm + 32, :]
    state_s[...] = carry['st']

    @pl.when(t == nt - 1)
    def _():
        stnew_ref[0] = state_s[...]
        bufnew_ref[0] = shift_s[0, 0:32, :]


def _even_layer(x, st_t, buf32, w, tm):
    bsz, L, _ = x.shape
    nt = L // tm
    full = lambda a: pl.BlockSpec(a.shape, lambda b, t: (0,) * a.ndim)
    weights = (w['win'], w['wlr'], w['blr'], w['ggla'], w['wdw'], w['bdw'], w['gcn'], w['bcn'], w['wout'],
               w['lng'], w['lnb'], *_gla_constants(min(tm, SUB_TILE)), w['bdk'], w['bdv'], w['bdst'])
    return pl.pallas_call(
        functools.partial(_even_kernel_pipelined, tm=tm, sub=min(tm, SUB_TILE)),
        grid=(bsz, nt),
        in_specs=[pl.BlockSpec((1, tm, D_MODEL), lambda b, t: (b, t, 0)),
                  pl.BlockSpec((1, H_A * DV_A, H_A * DK_A), lambda b, t: (b, 0, 0)),
                  pl.BlockSpec((1, 32, D_B), lambda b, t: (b, 0, 0))] + [full(a) for a in weights],
        out_specs=[pl.BlockSpec((1, tm, D_MODEL), lambda b, t: (b, t, 0)),
                   pl.BlockSpec((1, H_A * DV_A, H_A * DK_A), lambda b, t: (b, 0, 0)),
                   pl.BlockSpec((1, 32, D_B), lambda b, t: (b, 0, 0))],
        out_shape=[jax.ShapeDtypeStruct((bsz, L, D_MODEL), F32),
                   jax.ShapeDtypeStruct((bsz, H_A * DV_A, H_A * DK_A), F32),
                   jax.ShapeDtypeStruct((bsz, 32, D_B), F32)],
        scratch_shapes=[pltpu.VMEM((tm, E_END), F32),
                        pltpu.VMEM((8, tm + 32, D_B), F32),
                        pltpu.VMEM((H_A * DV_A, H_A * DK_A), F32),
                        pltpu.VMEM((tm, D_MODEL), BF16)],
        compiler_params=pltpu.CompilerParams(dimension_semantics=("arbitrary", "arbitrary"),
                                             vmem_limit_bytes=VMEM_LIMIT),
        name="even_mixer",
    )(x, st_t, buf32, *weights)


def _odd_kernel_pipelined(x_ref, h0_ref, st0_ref, buf0_ref, win_ref, wcv_ref, bcv_ref, wg_ref, bg_ref, lam_ref,
                          alog_ref, dtb_ref, gdl_ref, wout_ref, lng_ref, lnb_ref, bd_ref, bdk_ref, tri_ref, rest_ref,
                          y_ref, hnew_ref, stnew_ref, bufnew_ref,
                          cbuf_s, rest_s, cv_s, a_s, bx_s, hrow_s, state_s, cat_s, *, tm, sub):
    t = pl.program_id(1)
    nt = pl.num_programs(1)

    @pl.when(t == 0)
    def _():
        hrow_s[...] = h0_ref[0]
        state_s[...] = st0_ref[0]
        cbuf_s[0:8, :] = buf0_ref[0]

    nc = sub // CHUNK
    chunks = [slice(c * CHUNK, (c + 1) * CHUNK) for c in range(nc)]
    heads = [slice(h * CHUNK, (h + 1) * CHUNK) for h in range(H_D)]
    row = lax.broadcasted_iota(jnp.int32, (sub, H_D * CHUNK), 0) % CHUNK
    col = lax.broadcasted_iota(jnp.int32, (sub, H_D * CHUNK), 1) % CHUNK
    causal = (row >= col).astype(F32)
    strict = (row > col).astype(F32)
    eye = (row == col).astype(F32)
    sublane = lax.broadcasted_iota(jnp.int32, (8, D_C), 0)
    qoff, koff, voff = D_C, D_C + H_D * DK_D, D_C + 2 * H_D * DK_D
    carry = dict(h=jnp.broadcast_to(hrow_s[...], (8, D_C)), st=[state_s[h] for h in range(H_D)])
    vals = [dict() for _ in range(tm // sub)]

    def per_chunk(fn):
        return jnp.concatenate([fn(cs) for cs in chunks], axis=0)

    def blockdiag(m):
        return jnp.tile(m, (H_D, 1)) * bd_ref[...]

    def proj(i):
        v, r0 = vals[i], i * sub
        def first():
            v['x'] = x_ref[0, r0:r0 + sub, :]
            v['xb'] = v['x'].astype(BF16)
        def conv_cols(c0):
            def f():
                cbuf_s[8 + r0:8 + r0 + sub, c0:c0 + D_C] = _dot(v['xb'], win_ref[:, c0:c0 + D_C])
            return f
        def rest_cols(c0, c1):
            def f():
                rest_s[r0:r0 + sub, c0 - O_GC:c1 - O_GC] = _dot(v['xb'], win_ref[:, c0:c1])
            return f
        return [first] + [conv_cols(c0) for c0 in range(O_CONV, O_GC, D_C)] + [
            rest_cols(O_GC, O_Z), rest_cols(O_Z, O_END)]

    def conv(i):
        r0, rb = i * sub, 32
        def unit(rr, cb):
            def f():
                cs = slice(cb * D_C, (cb + 1) * D_C)
                acc = jnp.broadcast_to(bcv_ref[:, cs], (rb, D_C))
                for j in range(W_S):
                    acc = acc + wcv_ref[j:j + 1, cs] * cbuf_s[rr + 5 + j:rr + 5 + j + rb, cs]
                cv_s[rr:rr + rb, cs] = acc if cb == 0 else _silu(acc)
            return f
        return [unit(rr, cb) for rr in range(r0, r0 + sub, rb) for cb in range(CONV_ODD // D_C)]

    def lru_prep(i):
        r0 = i * sub
        def f():
            xc = cv_s[r0:r0 + sub, 0:D_C]
            gates = _dot(xc.astype(BF16), wg_ref[...]) + bg_ref[...]
            log_a = LRU_C * _sigmoid(gates[:, 0:D_C]) * (-_softplus(-lam_ref[...]))
            av = jnp.exp(log_a)
            one_m_a2 = -jnp.tanh(log_a) * (jnp.exp(2.0 * log_a) + 1.0)
            bv = jnp.sqrt(one_m_a2) * (_sigmoid(gates[:, D_C:2 * D_C]) * xc)
            a_s[r0:r0 + sub, :] = av
            bx_s[r0:r0 + sub, :] = bv
        return [f]

    def lru_scan(i):
        r0 = i * sub
        def group(rr):
            def f():
                a8, b8 = a_s[rr:rr + 8, :], bx_s[rr:rr + 8, :]
                for d in (1, 2, 4):
                    keep = sublane >= d
                    a_up = jnp.where(keep, pltpu.roll(a8, d, 0), 1.0)
                    b_up = jnp.where(keep, pltpu.roll(b8, d, 0), 0.0)
                    b8 = a8 * b_up + b8
                    a8 = a8 * a_up
                hg = a8 * carry['h'] + b8
                bx_s[rr:rr + 8, :] = hg
                carry['h'] = jnp.broadcast_to(hg[7:8, :], (8, D_C))
            return f
        def finish():
            cat_s[r0:r0 + sub, 0:D_C] = (bx_s[r0:r0 + sub, :] * jax.nn.gelu(rest_s[r0:r0 + sub, 0:D_C])).astype(BF16)
        return [group(rr) for rr in range(r0, r0 + sub, 8)] + [finish]

    def delta_prep(i):
        v, r0 = vals[i], i * sub
        rows = slice(r0, r0 + sub)
        def gates_f():
            ba = rest_s[rows, O_BA - O_GC:O_END - O_GC]
            v['beta'] = _sigmoid(ba)
            g_all = -jnp.exp(alog_ref[...]) * _softplus(ba + dtb_ref[...])
            v['gcum'] = _split_dot(tri_ref[...], g_all)
            v['grest'] = _split_dot(rest_ref[...], g_all)
            gb = jnp.concatenate([jnp.broadcast_to(g_all[:, H_D + h:H_D + h + 1], (sub, CHUNK)) for h in range(H_D)],
                                 axis=1)
            v['edm'] = jnp.exp(_split_dot(tri_ref[...], gb * strict))
            for key in ('kn', 'kb', 'qn', 'rhs', 'qg', 'kg', 'eg'):
                v[key] = []
        def head(h):
            def f():
                qh = cv_s[rows, qoff + h * DK_D:qoff + (h + 1) * DK_D]
                kh = cv_s[rows, koff + h * DK_D:koff + (h + 1) * DK_D]
                vh = cv_s[rows, voff + h * DV_D:voff + (h + 1) * DV_D]
                qn = qh * lax.rsqrt(jnp.sum(qh * qh, axis=-1, keepdims=True) + 1e-6) * (DK_D ** -0.5)
                kn = kh * lax.rsqrt(jnp.sum(kh * kh, axis=-1, keepdims=True) + 1e-6)
                beta = v['beta'][:, h:h + 1]
                gc = v['gcum'][:, H_D + h:H_D + h + 1]
                gr = v['grest'][:, H_D + h:H_D + h + 1]
                egc = jnp.exp(gc)
                kb = kn * beta
                v['kn'].append(kn.astype(BF16))
                v['kb'].append(kb.astype(BF16))
                v['qn'].append(qn.astype(BF16))
                v['rhs'].append(jnp.concatenate([vh * beta, kb * egc], axis=-1).astype(BF16))
                v['qg'].append((qn * egc).astype(BF16))
                v['kg'].append((kn * jnp.exp(gr)).astype(BF16))
                v['eg'].append(jnp.exp(gc + gr))
            return f
        return [gates_f] + [head(h) for h in range(H_D)]

    def delta_scores(i):
        v = vals[i]
        def f():
            kb_all = jnp.concatenate(v['kb'], axis=1)
            qn_all = jnp.concatenate(v['qn'], axis=1)
            kn_all = jnp.concatenate(v['kn'], axis=1)
            aq = [_dot_nt(jnp.concatenate([kb_all[cs], qn_all[cs]], axis=0),
                          jnp.tile(kn_all[cs], (H_D, 1)) * bdk_ref[...]) for cs in chunks]
            n = -(jnp.concatenate([a[0:CHUNK] for a in aq], axis=0) * v['edm'] * strict)
            v['attn'] = (jnp.concatenate([a[CHUNK:2 * CHUNK] for a in aq], axis=0) * v['edm'] * causal).astype(BF16)
            v['tt'] = eye + n
            v['pw'] = n
        return [f]

    def delta_inverse(i):
        v = vals[i]
        def square():
            pwb = v['pw'].astype(BF16)
            v['pw'] = per_chunk(lambda cs: _dot(pwb[cs], blockdiag(pwb[cs])))
        def extend():
            pwb, ttb = v['pw'].astype(BF16), v['tt'].astype(BF16)
            v['tt'] = v['tt'] + per_chunk(lambda cs: _dot(pwb[cs], blockdiag(ttb[cs])))
        return [square, extend] * 5

    def delta_state(i):
        v, r0 = vals[i], i * sub
        def solve():
            ttb = v['tt'].astype(BF16)
            v['sol'] = [[_dot(ttb[cs, hs], v['rhs'][h][cs]) for h, hs in enumerate(heads)] for cs in chunks]
            v['o'] = [[] for _ in range(H_D)]
        def step(c):
            def f():
                cs, st, sol = chunks[c], carry['st'], v['sol'][c]
                stb = [s.astype(BF16) for s in st]
                ws = [_dot(jnp.concatenate([sol[h][:, DV_D:].astype(BF16), v['qg'][h][cs]], axis=0), stb[h])
                      for h in range(H_D)]
                vnb = [(sol[h][:, 0:DV_D] - ws[h][0:CHUNK]).astype(BF16) for h in range(H_D)]
                for h in range(H_D):
                    v['o'][h].append(ws[h][CHUNK:2 * CHUNK] + _dot(v['attn'][cs, heads[h]], vnb[h]))
                carry['st'] = [v['eg'][h][c * CHUNK:c * CHUNK + 1, :] * st[h] + _dot_tn(v['kg'][h][cs], vnb[h])
                               for h in range(H_D)]
            return f
        return [solve] + [step(c) for c in range(nc)]

    def output(i):
        v, r0 = vals[i], i * sub
        rows = slice(r0, r0 + sub)
        def head(h):
            def f():
                o = jnp.concatenate(v['o'][h], axis=0)
                zg = rest_s[rows, O_Z - O_GC + h * DV_D:O_Z - O_GC + (h + 1) * DV_D]
                od = _rmsnorm(o) * gdl_ref[:, h * DV_D:(h + 1) * DV_D] * _silu(zg)
                cat_s[rows, D_C + h * DV_D:D_C + (h + 1) * DV_D] = od.astype(BF16)
            return f
        def final():
            y = _dot(cat_s[rows, :], wout_ref[...])
            y_ref[0, rows, :] = _layernorm(ALPHA * v['x'] + y, lng_ref[...], lnb_ref[...])
        return [head(h) for h in range(H_D)] + [final]

    if tm // sub == 1:
        for stage in (proj, conv, lru_prep):
            _emit(stage(0))
        _emit(_spread(lru_scan(0), delta_prep(0)))
        for stage in (delta_scores, delta_inverse, delta_state, output):
            _emit(stage(0))
    else:
        _emit(proj(0))
        _emit(_spread(conv(0), proj(1)))
        _emit(lru_prep(0))
        _emit(_spread(lru_scan(0), delta_prep(0)))
        _emit(delta_scores(0))
        _emit(_spread(conv(1), delta_inverse(0)))
        _emit(lru_prep(1))
        _emit(_spread(_spread(lru_scan(1), delta_prep(1)), delta_state(0)))
        _emit(delta_scores(1))
        _emit(_spread(delta_inverse(1), output(0)))
        _emit(delta_state(1))
        _emit(output(1))
    cbuf_s[0:8, :] = cbuf_s[tm:tm + 8, :]
    hrow_s[...] = carry['h'][0:1, :]
    for h in range(H_D):
        state_s[h] = carry['st'][h]

    @pl.when(t == nt - 1)
    def _():
        hnew_ref[0] = hrow_s[...]
        stnew_ref[0] = state_s[...]
        bufnew_ref[0] = cbuf_s[0:8, :]


def _chunk_sum_matrices(tm):
    r = np.arange(tm)
    same = (r[:, None] // CHUNK) == (r[None, :] // CHUNK)
    tri = same & (r[None, :] <= r[:, None])
    rest = same & (r[None, :] > r[:, None])
    return jnp.asarray(tri.astype(np.float32), BF16), jnp.asarray(rest.astype(np.float32), BF16)


def _odd_layer(x, h0, st0, buf8, w, tm):
    bsz, L, _ = x.shape
    nt = L // tm
    full = lambda a: pl.BlockSpec(a.shape, lambda b, t: (0,) * a.ndim)
    weights = (w['win'], w['wcv'], w['bcv'], w['wg'], w['bg'], w['lam'], w['alog'], w['dtb'], w['gdl'], w['wout'],
               w['lng'], w['lnb'], w['bd'], w['bdk'], *_chunk_sum_matrices(min(tm, SUB_TILE)))
    return pl.pallas_call(
        functools.partial(_odd_kernel_pipelined, tm=tm, sub=min(tm, SUB_TILE)),
        grid=(bsz, nt),
        in_specs=[pl.BlockSpec((1, tm, D_MODEL), lambda b, t: (b, t, 0)),
                  pl.BlockSpec((1, 1, D_C), lambda b, t: (b, 0, 0)),
                  pl.BlockSpec((1, H_D, DK_D, DV_D), lambda b, t: (b, 0, 0, 0)),
                  pl.BlockSpec((1, 8, CONV_ODD), lambda b, t: (b, 0, 0))] + [full(a) for a in weights],
        out_specs=[pl.BlockSpec((1, tm, D_MODEL), lambda b, t: (b, t, 0)),
                   pl.BlockSpec((1, 1, D_C), lambda b, t: (b, 0, 0)),
                   pl.BlockSpec((1, H_D, DK_D, DV_D), lambda b, t: (b, 0, 0, 0)),
                   pl.BlockSpec((1, 8, CONV_ODD), lambda b, t: (b, 0, 0))],
        out_shape=[jax.ShapeDtypeStruct((bsz, L, D_MODEL), F32),
                   jax.ShapeDtypeStruct((bsz, 1, D_C), F32),
                   jax.ShapeDtypeStruct((bsz, H_D, DK_D, DV_D), F32),
                   jax.ShapeDtypeStruct((bsz, 8, CONV_ODD), F32)],
        scratch_shapes=[pltpu.VMEM((tm + 8, CONV_ODD), F32),
                        pltpu.VMEM((tm, O_END - O_GC), F32),
                        pltpu.VMEM((tm, CONV_ODD), F32),
                        pltpu.VMEM((tm, D_C), F32),
                        pltpu.VMEM((tm, D_C), F32),
                        pltpu.VMEM((1, D_C), F32),
                        pltpu.VMEM((H_D, DK_D, DV_D), F32),
                        pltpu.VMEM((tm, D_MODEL), BF16)],
        compiler_params=pltpu.CompilerParams(dimension_semantics=("arbitrary", "arbitrary"),
                                             vmem_limit_bytes=VMEM_LIMIT),
        name="odd_mixer",
    )(x, h0, st0, buf8, *weights)


def _ffn_kernel(x_ref, buf0_ref, wup_ref, wdw_ref, bdw_ref, wdown_ref, lng_ref, lnb_ref,
                y_ref, bufnew_ref, hbuf_s, *, tm):
    t = pl.program_id(1)
    nt = pl.num_programs(1)

    @pl.when(t == 0)
    def _():
        hbuf_s[0:8, :] = buf0_ref[0]

    x = x_ref[0]
    xb = x.astype(BF16)
    acc = jnp.zeros((tm, D_MODEL), F32)
    for c0 in range(0, D_FF, FFN_COLS):
        cs = slice(c0, min(c0 + FFN_COLS, D_FF))
        if cs.stop <= FFN_MAIN:
            hg = _dot(xb, wup_ref[:, cs])
            hv = _dot(xb, wup_ref[:, FFN_MAIN + cs.start:FFN_MAIN + cs.stop])
        else:
            both = _dot(xb, wup_ref[:, 2 * FFN_MAIN:2 * D_FF])
            hg, hv = both[:, 0:D_FF - FFN_MAIN], both[:, D_FF - FFN_MAIN:]
        hbuf_s[8:8 + tm, cs] = hg
        cv = (bdw_ref[:, cs] + wdw_ref[0:1, cs] * hbuf_s[6:6 + tm, cs] + wdw_ref[1:2, cs] * hbuf_s[7:7 + tm, cs]
              + wdw_ref[2:3, cs] * hg)
        act = jax.nn.gelu(cv) * hv
        acc = acc + _dot(act.astype(BF16), wdown_ref[cs, :])
    hbuf_s[0:8, :] = hbuf_s[tm:tm + 8, :]
    y_ref[0] = _layernorm(ALPHA * x + acc, lng_ref[...], lnb_ref[...])

    @pl.when(t == nt - 1)
    def _():
        bufnew_ref[0] = hbuf_s[0:8, :]


def _ffn_layer(x, buf8, w, tm):
    bsz, L, _ = x.shape
    nt = L // tm
    full = lambda a: pl.BlockSpec(a.shape, lambda b, t: (0,) * a.ndim)
    weights = (w['wup'], w['wdw'], w['bdw'], w['wdown'], w['lng'], w['lnb'])
    return pl.pallas_call(
        functools.partial(_ffn_kernel, tm=tm),
        grid=(bsz, nt),
        in_specs=[pl.BlockSpec((1, tm, D_MODEL), lambda b, t: (b, t, 0)),
                  pl.BlockSpec((1, 8, D_FF), lambda b, t: (b, 0, 0))] + [full(a) for a in weights],
        out_specs=[pl.BlockSpec((1, tm, D_MODEL), lambda b, t: (b, t, 0)),
                   pl.BlockSpec((1, 8, D_FF), lambda b, t: (b, 0, 0))],
        out_shape=[jax.ShapeDtypeStruct((bsz, L, D_MODEL), F32),
                   jax.ShapeDtypeStruct((bsz, 8, D_FF), F32)],
        scratch_shapes=[pltpu.VMEM((tm + 8, D_FF), F32)],
        compiler_params=pltpu.CompilerParams(dimension_semantics=("arbitrary", "arbitrary"),
                                             vmem_limit_bytes=VMEM_LIMIT),
        name="conv_ffn",
    )(x, buf8, *weights)


def _row(v):
    return v.reshape(1, -1).astype(F32)


def _pad_rows(a, rows):
    return jnp.pad(a, ((0, rows - a.shape[0]), (0, 0)))


def _block_diag(w):
    h, d, _ = w.shape
    return jnp.einsum('hij,hg->higj', w, jnp.eye(h, dtype=w.dtype)).reshape(h * d, h * d)


def _prep_even(i, we_in, we_lr, be_lr, ge_gla, we_dw, be_dw, ge_cn, be_cn, we_out, ln1_g, ln1_b, l):
    w_in = we_in[i]
    lr0 = 2 * H_A * DK_A + 2 * H_A * DV_A
    win = jnp.concatenate([w_in[:, :lr0], w_in[:, lr0 + R_A:], w_in[:, lr0:lr0 + R_A],
                           jnp.zeros((D_MODEL, LANE - R_A), w_in.dtype)], axis=1).astype(BF16)
    return dict(win=win, wlr=_pad_rows(we_lr[i], LANE).astype(BF16), blr=_row(be_lr[i]), ggla=_row(ge_gla[i]),
                wdw=_pad_rows(we_dw[i], 32), bdw=_row(be_dw[i]), gcn=_row(ge_cn[i]), bcn=_row(be_cn[i]),
                wout=we_out[i].astype(BF16), lng=_row(ln1_g[l]), lnb=_row(ln1_b[l]),
                bdk=jnp.asarray(_block_mask(H_A * CHUNK, H_A * DK_A, CHUNK, DK_A), BF16),
                bdv=jnp.asarray(_block_mask(H_A * CHUNK, H_A * DV_A, CHUNK, DV_A), BF16),
                bdst=jnp.asarray(_block_mask(H_A * DV_A, H_A * DK_A, DV_A, DK_A), F32))


def _prep_odd(i, wo_in, wo_conv, bo_conv, wo_rg, bo_rg, wo_ig, bo_ig, lam_lru, a_log, dt_bias, go_delta, wo_out,
              ln1_g, ln1_b, l):
    w_in = wo_in[i]
    win = jnp.concatenate([w_in, jnp.zeros((D_MODEL, O_END - w_in.shape[1]), w_in.dtype)], axis=1).astype(BF16)
    head_row = lambda v: jnp.pad(v.astype(F32), (H_D, LANE - 2 * H_D)).reshape(1, LANE)
    return dict(win=win, wcv=_pad_rows(wo_conv[i], 8), bcv=_row(bo_conv[i]),
                wg=jnp.concatenate([_block_diag(wo_rg[i]), _block_diag(wo_ig[i])], axis=1).astype(BF16),
                bg=_row(jnp.concatenate([bo_rg[i], bo_ig[i]])), lam=_row(lam_lru[i]),
                alog=head_row(a_log[i]), dtb=head_row(dt_bias[i]), gdl=_row(go_delta[i]),
                wout=wo_out[i].astype(BF16), lng=_row(ln1_g[l]), lnb=_row(ln1_b[l]),
                bd=jnp.asarray(_block_mask(H_D * CHUNK, H_D * CHUNK, CHUNK, CHUNK), BF16),
                bdk=jnp.asarray(_block_mask(H_D * CHUNK, H_D * DK_D, CHUNK, DK_D), BF16))


def _prep_ffn(l, w_up, w_fdw, b_fdw, w_down, ln2_g, ln2_b):
    wg, wv = w_up[l][:, :D_FF], w_up[l][:, D_FF:]
    wup = jnp.concatenate([wg[:, :FFN_MAIN], wv[:, :FFN_MAIN], wg[:, FFN_MAIN:], wv[:, FFN_MAIN:]], axis=1)
    return dict(wup=wup.astype(BF16), wdw=_pad_rows(w_fdw[l], 8), bdw=_row(b_fdw[l]),
                wdown=w_down[l].astype(BF16), lng=_row(ln2_g[l]), lnb=_row(ln2_b[l]))


def _front_pad(buf, rows):
    return jnp.pad(buf, ((0, 0), (rows - buf.shape[1], 0), (0, 0)))


def _gla_state_to_blockdiag(s):
    s_t = jnp.swapaxes(s, 2, 3)
    return jnp.concatenate([jnp.pad(s_t[:, h], ((0, 0), (0, 0), (h * DK_A, (H_A - 1 - h) * DK_A)))
                            for h in range(H_A)], axis=1)


def _gla_state_from_blockdiag(s_bd):
    bsz = s_bd.shape[0]
    s5 = s_bd.reshape(bsz, H_A, DV_A, H_A, DK_A)
    return jnp.stack([jnp.swapaxes(s5[:, h, :, h, :], 1, 2) for h in range(H_A)], axis=1)


def _trunk(x, states, mix_w, ffn_w):
    L = x.shape[1]
    new_states = []
    for l in range(DEPTH):
        st = states[l]
        if l % 2 == 0:
            x, s_bd, buf = _even_layer(x, _gla_state_to_blockdiag(st[0]), _front_pad(st[1], 32), mix_w[l],
                                       min(L, EVEN_TILE))
            mix_new = (_gla_state_from_blockdiag(s_bd), buf[:, 32 - (W_B - 1):])
        else:
            x, h, s, buf = _odd_layer(x, st[0][:, None, :], st[1], _front_pad(st[2], 8), mix_w[l], min(L, ODD_TILE))
            mix_new = (h[:, 0], s, buf[:, 8 - (W_S - 1):])
        x, fbuf = _ffn_layer(x, _front_pad(st[-1], 8), ffn_w[l], min(L, FFN_TILE))
        new_states.append((*mix_new, fbuf[:, 8 - (W_F - 1):]))
    return x, new_states


def _zero_states(bsz):
    z = lambda *s: jnp.zeros((bsz,) + s, F32)
    return [(z(H_A, DK_A, DV_A), z(W_B - 1, D_B), z(W_F - 1, D_FF)) if l % 2 == 0 else
            (z(D_C), z(H_D, DK_D, DV_D), z(W_S - 1, CONV_ODD), z(W_F - 1, D_FF)) for l in range(DEPTH)]


def kernel(x_prompt, x_sample, state_l0_gla, cache_l0_dwconv, cache_l0_ffn, state_l1_lru, state_l1_delta, cache_l1_conv, cache_l1_ffn, state_l2_gla, cache_l2_dwconv, cache_l2_ffn, state_l3_lru, state_l3_delta, cache_l3_conv, cache_l3_ffn, we_in, we_lr, be_lr, ge_gla, we_dw, be_dw, ge_cn, be_cn, we_out, wo_in, wo_conv, bo_conv, wo_rg, bo_rg, wo_ig, bo_ig, lam_lru, a_log, dt_bias, go_delta, wo_out, w_up, w_fdw, b_fdw, w_down, ln1_g, ln1_b, ln2_g, ln2_b):
    mix_w = []
    for l in range(DEPTH):
        if l % 2 == 0:
            mix_w.append(_prep_even(l // 2, we_in, we_lr, be_lr, ge_gla, we_dw, be_dw, ge_cn, be_cn, we_out,
                                    ln1_g, ln1_b, l))
        else:
            mix_w.append(_prep_odd(l // 2, wo_in, wo_conv, bo_conv, wo_rg, bo_rg, wo_ig, bo_ig, lam_lru, a_log,
                                   dt_bias, go_delta, wo_out, ln1_g, ln1_b, l))
    ffn_w = [_prep_ffn(l, w_up, w_fdw, b_fdw, w_down, ln2_g, ln2_b) for l in range(DEPTH)]
    y_prompt, new_p = _trunk(x_prompt, _zero_states(x_prompt.shape[0]), mix_w, ffn_w)
    sample_states = [(state_l0_gla, cache_l0_dwconv, cache_l0_ffn),
                     (state_l1_lru, state_l1_delta, cache_l1_conv, cache_l1_ffn),
                     (state_l2_gla, cache_l2_dwconv, cache_l2_ffn),
                     (state_l3_lru, state_l3_delta, cache_l3_conv, cache_l3_ffn)]
    y_sample, new_s = _trunk(x_sample, sample_states, mix_w, ffn_w)
    flat = lambda ns: [a for layer in ns for a in layer]
    return (y_prompt, y_sample, *flat(new_p), *flat(new_s))
```

```python
import functools

import numpy as np
import jax
import jax.numpy as jnp
from jax import lax
from jax.experimental import pallas as pl
from jax.experimental.pallas import tpu as pltpu

F32 = jnp.float32
BF16 = jnp.bfloat16

D_MODEL = 1024
DEPTH = 4
CHUNK = 64
H_A, DK_A, DV_A, R_A, TAU_A = 4, 64, 128, 16, 16.0
D_B, W_B = 512, 31
D_C, H_C, DH_C, LRU_C = 512, 8, 64, 8.0
H_D, DK_D, DV_D, W_S = 4, 128, 128, 4
D_FF, W_F = 2688, 3
ALPHA = (2 * DEPTH) ** 0.25
EPS = 1e-5
CONV_ODD = D_C + 2 * H_D * DK_D + H_D * DV_D

LANE = 128
SUBLANES = 8
EVEN_TILE = 512
ODD_TILE = 512
FFN_TILE = 512
SUB_TILE = 256
VMEM_LIMIT = 56 * 1024 * 1024
MXU_TILE = 256
FFN_COLS = 5 * MXU_TILE
FFN_MAIN = D_FF // FFN_COLS * FFN_COLS
GLA_LEVELS = 6

E_Q, E_K, E_V, E_G, E_GA, E_GB, E_LR, E_END = 0, 256, 512, 1024, 1536, 2048, 2560, 2688
O_CONV, O_GC, O_Z, O_BA, O_END = 0, 2048, 2560, 3072, 3200


def _dot(a, b):
    return jnp.dot(a, b, preferred_element_type=F32)


def _dot_nt(a, b):
    return lax.dot_general(a, b, (((1,), (1,)), ((), ())), preferred_element_type=F32)


def _dot_tn(a, b):
    return lax.dot_general(a, b, (((0,), (0,)), ((), ())), preferred_element_type=F32)


def _split_dot(m, x):
    hi = x.astype(BF16)
    lo = (x - hi.astype(F32)).astype(BF16)
    return _dot(m, hi) + _dot(m, lo)


def _sigmoid(x):
    return jax.nn.sigmoid(x)


def _silu(x):
    return x * jax.nn.sigmoid(x)


def _softplus(x):
    return jnp.maximum(x, 0.0) + jnp.log(1.0 + jnp.exp(-jnp.abs(x)))


def _layernorm(x, g, b):
    mu = jnp.mean(x, axis=-1, keepdims=True)
    xc = x - mu
    var = jnp.mean(xc * xc, axis=-1, keepdims=True)
    return xc * lax.rsqrt(var + EPS) * g + b


def _rmsnorm(x):
    return x * lax.rsqrt(jnp.mean(x * x, axis=-1, keepdims=True) + EPS)


def _block_mask(rows, cols, rblk, cblk):
    r = np.arange(rows)[:, None] // rblk
    c = np.arange(cols)[None, :] // cblk
    return (r == c).astype(np.float32)


def _gla_constants(tm):
    c = CHUNK
    r = np.arange(c)
    tri = (r[None, :] <= r[:, None]).astype(np.float32)
    rest = (r[None, :] > r[:, None]).astype(np.float32)
    mats = [tri, rest]
    masks = [np.eye(c, dtype=np.float32)]
    for lvl in range(GLA_LEVELS):
        m = (c // 2) >> lvl
        anchor = (r // (2 * m)) * (2 * m) + m - 1
        mats.append(tri - tri[anchor])
        same = (r[:, None] // (2 * m)) == (r[None, :] // (2 * m))
        masks.append((same & ((r[:, None] % (2 * m)) >= m) & ((r[None, :] % (2 * m)) < m)).astype(np.float32))
    nc = tm // c
    call = np.concatenate([np.kron(np.eye(nc, dtype=np.float32), m) for m in mats], 0)
    return jnp.asarray(call, BF16), jnp.asarray(np.tile(np.stack(masks, 0), (1, nc, H_A)), F32)


def _spread(major, minor):
    out, j = [], 0
    for i, f in enumerate(major):
        out.append(f)
        while j < (i + 1) * len(minor) // len(major):
            out.append(minor[j])
            j += 1
    return out


def _emit(fns):
    for f in fns:
        f()


def _even_kernel_pipelined(x_ref, st0_ref, buf0_ref, win_ref, wlr_ref, blr_ref, ggla_ref, wdw_ref, bdw_ref, gcn_ref,
                           bcn_ref, wout_ref, lng_ref, lnb_ref, call_ref, masks_ref, bdk_ref, bdv_ref, bdst_ref,
                           y_ref, stnew_ref, bufnew_ref,
                           proj_s, shift_s, state_s, cat_s, *, tm, sub):
    t = pl.program_id(1)
    nt = pl.num_programs(1)

    @pl.when(t == 0)
    def _():
        state_s[...] = st0_ref[0]
        shift_s[0, 0:32, :] = buf0_ref[0]

    nc = sub // CHUNK
    chunks = [slice(c * CHUNK, (c + 1) * CHUNK) for c in range(nc)]
    carry = dict(st=state_s[...])
    vals = [dict() for _ in range(tm // sub)]

    def per_chunk(fn):
        return jnp.concatenate([fn(cs) for cs in chunks], axis=0)

    def proj(i):
        v, r0 = vals[i], i * sub
        def first():
            v['x'] = x_ref[0, r0:r0 + sub, :]
            v['xb'] = v['x'].astype(BF16)
        def cols(c0, c1):
            def f():
                proj_s[r0:r0 + sub, c0:c1] = _dot(v['xb'], win_ref[:, c0:c1])
            return f
        return [first] + [cols(c0, c1) for c0, c1 in ((E_GA, E_GB), (E_GB, E_LR), (E_Q, E_V), (E_V, E_G), (E_G, E_GA),
                                                      (E_LR, E_END))]

    def glu(i):
        r0 = i * sub
        def f():
            rows = slice(r0, r0 + sub)
            shift_s[0, 32 + r0:32 + r0 + sub, :] = proj_s[rows, E_GA:E_GB] * _sigmoid(proj_s[rows, E_GB:E_LR])
            lo = 0 if i == 0 else r0 + 24
            for s in range(1, 8):
                shift_s[s, lo:r0 + sub + 24, :] = shift_s[0, lo + s:r0 + sub + 24 + s, :]
        return [f]

    def conv(i):
        r0, rb = i * sub, 32
        def unit(rr):
            def f():
                acc = jnp.broadcast_to(bdw_ref[...], (rb, D_B))
                for j in range(W_B):
                    off = rr + 2 + j
                    acc = acc + wdw_ref[j:j + 1, :] * shift_s[off % 8, off - off % 8:off - off % 8 + rb, :]
                ob = _silu(_layernorm(acc, gcn_ref[...], bcn_ref[...]))
                cat_s[rr:rr + rb, D_B:2 * D_B] = ob.astype(BF16)
            return f
        return [unit(rr) for rr in range(r0, r0 + sub, rb)]

    def gla_prep(i):
        v, r0 = vals[i], i * sub
        rows = slice(r0, r0 + sub)
        def f():
            v['q'] = proj_s[rows, E_Q:E_K] * (DK_A ** -0.5)
            v['k'] = proj_s[rows, E_K:E_V]
            v['vb'] = proj_s[rows, E_V:E_G].astype(BF16)
            z = _dot(proj_s[rows, E_LR:E_END].astype(BF16), wlr_ref[...]) + blr_ref[...]
            la = -_softplus(-z) * (1.0 / TAU_A)
            v['e'] = _split_dot(call_ref[...], la)
        def g():
            bcum, brest = v['e'][0:sub], v['e'][sub:2 * sub]
            v['qe'] = (v['q'] * jnp.exp(bcum)).astype(BF16)
            kr = (v['k'] * jnp.exp(brest)).astype(BF16)
            v['dlast'] = jnp.exp(bcum + brest)
            v['kv'] = [bdst_ref[...] * _dot_tn(v['vb'][cs], kr[cs]) for cs in chunks]
            v['p'] = jnp.zeros((sub, H_A * CHUNK), F32)
        return [f, g]

    def gla_scores(i):
        v = vals[i]
        def level(lvl):
            def f():
                if lvl == 0:
                    qf, kf = v['q'].astype(BF16), v['k'].astype(BF16)
                else:
                    fac = jnp.exp(-jnp.abs(v['e'][(1 + lvl) * sub:(2 + lvl) * sub]))
                    qf, kf = (v['q'] * fac).astype(BF16), (v['k'] * fac).astype(BF16)
                v['p'] = v['p'] + masks_ref[lvl] * per_chunk(
                    lambda cs: _dot_nt(qf[cs], jnp.tile(kf[cs], (H_A, 1)) * bdk_ref[...]))
            return f
        return [level(lvl) for lvl in range(GLA_LEVELS + 1)]

    def gla_state(i):
        v = vals[i]
        def f():
            pb = v['p'].astype(BF16)
            o_intra = [_dot(pb[cs], jnp.tile(v['vb'][cs], (H_A, 1)) * bdv_ref[...]) for cs in chunks]
            st, sts = carry['st'], []
            for c in range(nc):
                sts.append(st.astype(BF16))
                st = st * v['dlast'][c * CHUNK:c * CHUNK + 1, :] + v['kv'][c]
            carry['st'] = st
            v['o'] = jnp.concatenate([_dot_nt(v['qe'][cs], sts[c]) + o_intra[c] for c, cs in enumerate(chunks)],
                                     axis=0)
        return [f]

    def output(i):
        v, r0 = vals[i], i * sub
        rows = slice(r0, r0 + sub)
        def head(h):
            def f():
                hs = slice(h * DV_A, (h + 1) * DV_A)
                oa = _rmsnorm(v['o'][:, hs]) * ggla_ref[:, hs] * _silu(proj_s[rows, E_G + h * DV_A:E_G + (h + 1) * DV_A])
                cat_s[rows, hs] = oa.astype(BF16)
            return f
        def final():
            y = _dot(cat_s[rows, :], wout_ref[...])
            y_ref[0, rows, :] = _layernorm(ALPHA * v['x'] + y, lng_ref[...], lnb_ref[...])
        return [head(h) for h in range(H_A)] + [final]

    if tm // sub == 1:
        for stage in (proj, glu, conv, gla_prep, gla_scores, gla_state, output):
            _emit(stage(0))
    else:
        _emit(proj(0)[:3])
        _emit(glu(0))
        _emit(_spread(conv(0), proj(0)[3:] + proj(1)[:3]))
        _emit(glu(1))
        _emit(_spread(conv(1), proj(1)[3:] + gla_prep(0)))
        _emit(gla_scores(0))
        _emit(gla_state(0))
        _emit(_spread(gla_prep(1) + gla_scores(1), output(0)))
        _emit(gla_state(1))
        _emit(output(1))
    shift_s[0, 0:32, :] = shift_s[0, tm:tm + 32, :]
    state_s[...] = carry['st']

    @pl.when(t == nt - 1)
    def _():
        stnew_ref[0] = state_s[...]
        bufnew_ref[0] = shift_s[0, 0:32, :]


def _even_layer(x, st_t, buf32, w, tm):
    bsz, L, _ = x.shape
    nt = L // tm
    full = lambda a: pl.BlockSpec(a.shape, lambda b, t: (0,) * a.ndim)
    weights = (w['win'], w['wlr'], w['blr'], w['ggla'], w['wdw'], w['bdw'], w['gcn'], w['bcn'], w['wout'],
               w['lng'], w['lnb'], *_gla_constants(min(tm, SUB_TILE)), w['bdk'], w['bdv'], w['bdst'])
    return pl.pallas_call(
        functools.partial(_even_kernel_pipelined, tm=tm, sub=min(tm, SUB_TILE)),
        grid=(bsz, nt),
        in_specs=[pl.BlockSpec((1, tm, D_MODEL), lambda b, t: (b, t, 0)),
                  pl.BlockSpec((1, H_A * DV_A, H_A * DK_A), lambda b, t: (b, 0, 0)),
                  pl.BlockSpec((1, 32, D_B), lambda b, t: (b, 0, 0))] + [full(a) for a in weights],
        out_specs=[pl.BlockSpec((1, tm, D_MODEL), lambda b, t: (b, t, 0)),
                   pl.BlockSpec((1, H_A * DV_A, H_A * DK_A), lambda b, t: (b, 0, 0)),
                   pl.BlockSpec((1, 32, D_B), lambda b, t: (b, 0, 0))],
        out_shape=[jax.ShapeDtypeStruct((bsz, L, D_MODEL), F32),
                   jax.ShapeDtypeStruct((bsz, H_A * DV_A, H_A * DK_A), F32),
                   jax.ShapeDtypeStruct((bsz, 32, D_B), F32)],
        scratch_shapes=[pltpu.VMEM((tm, E_END), F32),
                        pltpu.VMEM((8, tm + 32, D_B), F32),
                        pltpu.VMEM((H_A * DV_A, H_A * DK_A), F32),
                        pltpu.VMEM((tm, D_MODEL), BF16)],
        compiler_params=pltpu.CompilerParams(dimension_semantics=("arbitrary", "arbitrary"),
                                             vmem_limit_bytes=VMEM_LIMIT),
        name="even_mixer",
    )(x, st_t, buf32, *weights)


def _odd_kernel_pipelined(x_ref, h0_ref, st0_ref, buf0_ref, win_ref, wcv_ref, bcv_ref, wg_ref, bg_ref, lam_ref,
                          alog_ref, dtb_ref, gdl_ref, wout_ref, lng_ref, lnb_ref, bd_ref, bdk_ref, tri_ref, rest_ref,
                          y_ref, hnew_ref, stnew_ref, bufnew_ref,
                          cbuf_s, rest_s, cv_s, a_s, bx_s, hrow_s, state_s, cat_s, *, tm, sub):
    t = pl.program_id(1)
    nt = pl.num_programs(1)

    @pl.when(t == 0)
    def _():
        hrow_s[...] = h0_ref[0]
        state_s[...] = st0_ref[0]
        cbuf_s[0:8, :] = buf0_ref[0]

    nc = sub // CHUNK
    chunks = [slice(c * CHUNK, (c + 1) * CHUNK) for c in range(nc)]
    heads = [slice(h * CHUNK, (h + 1) * CHUNK) for h in range(H_D)]
    row = lax.broadcasted_iota(jnp.int32, (sub, H_D * CHUNK), 0) % CHUNK
    col = lax.broadcasted_iota(jnp.int32, (sub, H_D * CHUNK), 1) % CHUNK
    causal = (row >= col).astype(F32)
    strict = (row > col).astype(F32)
    eye = (row == col).astype(F32)
    sublane = lax.broadcasted_iota(jnp.int32, (SUBLANES, D_C), 0)
    qoff, koff, voff = D_C, D_C + H_D * DK_D, D_C + 2 * H_D * DK_D
    carry = dict(h=jnp.broadcast_to(hrow_s[...], (SUBLANES, D_C)), st=[state_s[h] for h in range(H_D)])
    vals = [dict() for _ in range(tm // sub)]

    def per_chunk(fn):
        return jnp.concatenate([fn(cs) for cs in chunks], axis=0)

    def blockdiag(m):
        return jnp.tile(m, (H_D, 1)) * bd_ref[...]

    def proj(i):
        v, r0 = vals[i], i * sub
        def first():
            v['x'] = x_ref[0, r0:r0 + sub, :]
            v['xb'] = v['x'].astype(BF16)
        def conv_cols(c0):
            def f():
                cbuf_s[8 + r0:8 + r0 + sub, c0:c0 + D_C] = _dot(v['xb'], win_ref[:, c0:c0 + D_C])
            return f
        def rest_cols(c0, c1):
            def f():
                rest_s[r0:r0 + sub, c0 - O_GC:c1 - O_GC] = _dot(v['xb'], win_ref[:, c0:c1])
            return f
        return [first] + [conv_cols(c0) for c0 in range(O_CONV, O_GC, D_C)] + [
            rest_cols(O_GC, O_Z), rest_cols(O_Z, O_END)]

    def conv(i):
        r0, rb = i * sub, 32
        def unit(rr, cb):
            def f():
                cs = slice(cb * D_C, (cb + 1) * D_C)
                acc = jnp.broadcast_to(bcv_ref[:, cs], (rb, D_C))
                for j in range(W_S):
                    acc = acc + wcv_ref[j:j + 1, cs] * cbuf_s[rr + 5 + j:rr + 5 + j + rb, cs]
                cv_s[rr:rr + rb, cs] = acc if cb == 0 else _silu(acc)
            return f
        return [unit(rr, cb) for rr in range(r0, r0 + sub, rb) for cb in range(CONV_ODD // D_C)]

    def lru_prep(i):
        r0 = i * sub
        def f():
            xc = cv_s[r0:r0 + sub, 0:D_C]
            gates = _dot(xc.astype(BF16), wg_ref[...]) + bg_ref[...]
            log_a = LRU_C * _sigmoid(gates[:, 0:D_C]) * (-_softplus(-lam_ref[...]))
            av = jnp.exp(log_a)
            one_m_a2 = -jnp.tanh(log_a) * (jnp.exp(2.0 * log_a) + 1.0)
            bv = jnp.sqrt(one_m_a2) * (_sigmoid(gates[:, D_C:2 * D_C]) * xc)
            a_s[r0:r0 + sub, :] = av
            bx_s[r0:r0 + sub, :] = bv
        return [f]

    def lru_scan(i):
        r0 = i * sub
        def group(rr):
            def f():
                a8, b8 = a_s[rr:rr + SUBLANES, :], bx_s[rr:rr + SUBLANES, :]
                for d in (1, 2, 4):
                    keep = sublane >= d
                    a_up = jnp.where(keep, pltpu.roll(a8, d, 0), 1.0)
                    b_up = jnp.where(keep, pltpu.roll(b8, d, 0), 0.0)
                    b8 = a8 * b_up + b8
                    a8 = a8 * a_up
                hg = a8 * carry['h'] + b8
                bx_s[rr:rr + SUBLANES, :] = hg
                carry['h'] = jnp.broadcast_to(hg[SUBLANES - 1:SUBLANES, :], (SUBLANES, D_C))
            return f
        def finish():
            cat_s[r0:r0 + sub, 0:D_C] = (bx_s[r0:r0 + sub, :] * jax.nn.gelu(rest_s[r0:r0 + sub, 0:D_C])).astype(BF16)
        return [group(rr) for rr in range(r0, r0 + sub, SUBLANES)] + [finish]

    def delta_prep(i):
        v, r0 = vals[i], i * sub
        rows = slice(r0, r0 + sub)
        def gates_f():
            ba = rest_s[rows, O_BA - O_GC:O_END - O_GC]
            v['beta'] = _sigmoid(ba)
            g_all = -jnp.exp(alog_ref[...]) * _softplus(ba + dtb_ref[...])
            v['gcum'] = _split_dot(tri_ref[...], g_all)
            v['grest'] = _split_dot(rest_ref[...], g_all)
            gb = jnp.concatenate([jnp.broadcast_to(g_all[:, H_D + h:H_D + h + 1], (sub, CHUNK)) for h in range(H_D)],
                                 axis=1)
            v['edm'] = jnp.exp(_split_dot(tri_ref[...], gb * strict))
            for key in ('kn', 'kb', 'qn', 'rhs', 'qg', 'kg', 'eg'):
                v[key] = []
        def head(h):
            def f():
                qh = cv_s[rows, qoff + h * DK_D:qoff + (h + 1) * DK_D]
                kh = cv_s[rows, koff + h * DK_D:koff + (h + 1) * DK_D]
                vh = cv_s[rows, voff + h * DV_D:voff + (h + 1) * DV_D]
                qn = qh * lax.rsqrt(jnp.sum(qh * qh, axis=-1, keepdims=True) + 1e-6) * (DK_D ** -0.5)
                kn = kh * lax.rsqrt(jnp.sum(kh * kh, axis=-1, keepdims=True) + 1e-6)
                beta = v['beta'][:, h:h + 1]
                gc = v['gcum'][:, H_D + h:H_D + h + 1]
                gr = v['grest'][:, H_D + h:H_D + h + 1]
                egc = jnp.exp(gc)
                kb = kn * beta
                v['kn'].append(kn.astype(BF16))
                v['kb'].append(kb.astype(BF16))
                v['qn'].append(qn.astype(BF16))
                v['rhs'].append(jnp.concatenate([vh * beta, kb * egc], axis=-1).astype(BF16))
                v['qg'].append((qn * egc).astype(BF16))
                v['kg'].append((kn * jnp.exp(gr)).astype(BF16))
                v['eg'].append(jnp.exp(gc + gr))
            return f
        return [gates_f] + [head(h) for h in range(H_D)]

    def delta_scores(i):
        v = vals[i]
        def f():
            kb_all = jnp.concatenate(v['kb'], axis=1)
            qn_all = jnp.concatenate(v['qn'], axis=1)
            kn_all = jnp.concatenate(v['kn'], axis=1)
            aq = [_dot_nt(jnp.concatenate([kb_all[cs], qn_all[cs]], axis=0),
                          jnp.tile(kn_all[cs], (H_D, 1)) * bdk_ref[...]) for cs in chunks]
            n = -(jnp.concatenate([a[0:CHUNK] for a in aq], axis=0) * v['edm'] * strict)
            v['attn'] = (jnp.concatenate([a[CHUNK:2 * CHUNK] for a in aq], axis=0) * v['edm'] * causal).astype(BF16)
            v['tt'] = eye + n
            v['pw'] = n
        return [f]

    def delta_inverse(i):
        v = vals[i]
        def square():
            pwb = v['pw'].astype(BF16)
            v['pw'] = per_chunk(lambda cs: _dot(pwb[cs], blockdiag(pwb[cs])))
        def extend():
            pwb, ttb = v['pw'].astype(BF16), v['tt'].astype(BF16)
            v['tt'] = v['tt'] + per_chunk(lambda cs: _dot(pwb[cs], blockdiag(ttb[cs])))
        return [square, extend] * 5

    def delta_state(i):
        v, r0 = vals[i], i * sub
        def solve():
            ttb = v['tt'].astype(BF16)
            v['sol'] = [[_dot(ttb[cs, hs], v['rhs'][h][cs]) for h, hs in enumerate(heads)] for cs in chunks]
            v['o'] = [[] for _ in range(H_D)]
        def step(c):
            def f():
                cs, st, sol = chunks[c], carry['st'], v['sol'][c]
                stb = [s.astype(BF16) for s in st]
                ws = [_dot(jnp.concatenate([sol[h][:, DV_D:].astype(BF16), v['qg'][h][cs]], axis=0), stb[h])
                      for h in range(H_D)]
                vnb = [(sol[h][:, 0:DV_D] - ws[h][0:CHUNK]).astype(BF16) for h in range(H_D)]
                for h in range(H_D):
                    v['o'][h].append(ws[h][CHUNK:2 * CHUNK] + _dot(v['attn'][cs, heads[h]], vnb[h]))
                carry['st'] = [v['eg'][h][c * CHUNK:c * CHUNK + 1, :] * st[h] + _dot_tn(v['kg'][h][cs], vnb[h])
                               for h in range(H_D)]
            return f
        return [solve] + [step(c) for c in range(nc)]

    def output(i):
        v, r0 = vals[i], i * sub
        rows = slice(r0, r0 + sub)
        def head(h):
            def f():
                o = jnp.concatenate(v['o'][h], axis=0)
                zg = rest_s[rows, O_Z - O_GC + h * DV_D:O_Z - O_GC + (h + 1) * DV_D]
                od = _rmsnorm(o) * gdl_ref[:, h * DV_D:(h + 1) * DV_D] * _silu(zg)
                cat_s[rows, D_C + h * DV_D:D_C + (h + 1) * DV_D] = od.astype(BF16)
            return f
        def final():
            y = _dot(cat_s[rows, :], wout_ref[...])
            y_ref[0, rows, :] = _layernorm(ALPHA * v['x'] + y, lng_ref[...], lnb_ref[...])
        return [head(h) for h in range(H_D)] + [final]

    if tm // sub == 1:
        for stage in (proj, conv, lru_prep):
            _emit(stage(0))
        _emit(_spread(lru_scan(0), delta_prep(0)))
        for stage in (delta_scores, delta_inverse, delta_state, output):
            _emit(stage(0))
    else:
        _emit(proj(0))
        _emit(_spread(conv(0), proj(1)))
        _emit(lru_prep(0))
        _emit(_spread(lru_scan(0), delta_prep(0)))
        _emit(delta_scores(0))
        _emit(_spread(conv(1), delta_inverse(0)))
        _emit(lru_prep(1))
        _emit(_spread(_spread(lru_scan(1), delta_prep(1)), delta_state(0)))
        _emit(delta_scores(1))
        _emit(_spread(delta_inverse(1), output(0)))
        _emit(delta_state(1))
        _emit(output(1))
    cbuf_s[0:8, :] = cbuf_s[tm:tm + 8, :]
    hrow_s[...] = carry['h'][0:1, :]
    for h in range(H_D):
        state_s[h] = carry['st'][h]

    @pl.when(t == nt - 1)
    def _():
        hnew_ref[0] = hrow_s[...]
        stnew_ref[0] = state_s[...]
        bufnew_ref[0] = cbuf_s[0:8, :]


def _chunk_sum_matrices(tm):
    r = np.arange(tm)
    same = (r[:, None] // CHUNK) == (r[None, :] // CHUNK)
    tri = same & (r[None, :] <= r[:, None])
    rest = same & (r[None, :] > r[:, None])
    return jnp.asarray(tri.astype(np.float32), BF16), jnp.asarray(rest.astype(np.float32), BF16)


def _odd_layer(x, h0, st0, buf8, w, tm):
    bsz, L, _ = x.shape
    nt = L // tm
    full = lambda a: pl.BlockSpec(a.shape, lambda b, t: (0,) * a.ndim)
    weights = (w['win'], w['wcv'], w['bcv'], w['wg'], w['bg'], w['lam'], w['alog'], w['dtb'], w['gdl'], w['wout'],
               w['lng'], w['lnb'], w['bd'], w['bdk'], *_chunk_sum_matrices(min(tm, SUB_TILE)))
    return pl.pallas_call(
        functools.partial(_odd_kernel_pipelined, tm=tm, sub=min(tm, SUB_TILE)),
        grid=(bsz, nt),
        in_specs=[pl.BlockSpec((1, tm, D_MODEL), lambda b, t: (b, t, 0)),
                  pl.BlockSpec((1, 1, D_C), lambda b, t: (b, 0, 0)),
                  pl.BlockSpec((1, H_D, DK_D, DV_D), lambda b, t: (b, 0, 0, 0)),
                  pl.BlockSpec((1, 8, CONV_ODD), lambda b, t: (b, 0, 0))] + [full(a) for a in weights],
        out_specs=[pl.BlockSpec((1, tm, D_MODEL), lambda b, t: (b, t, 0)),
                   pl.BlockSpec((1, 1, D_C), lambda b, t: (b, 0, 0)),
                   pl.BlockSpec((1, H_D, DK_D, DV_D), lambda b, t: (b, 0, 0, 0)),
                   pl.BlockSpec((1, 8, CONV_ODD), lambda b, t: (b, 0, 0))],
        out_shape=[jax.ShapeDtypeStruct((bsz, L, D_MODEL), F32),
                   jax.ShapeDtypeStruct((bsz, 1, D_C), F32),
                   jax.ShapeDtypeStruct((bsz, H_D, DK_D, DV_D), F32),
                   jax.ShapeDtypeStruct((bsz, 8, CONV_ODD), F32)],
        scratch_shapes=[pltpu.VMEM((tm + 8, CONV_ODD), F32),
                        pltpu.VMEM((tm, O_END - O_GC), F32),
                        pltpu.VMEM((tm, CONV_ODD), F32),
                        pltpu.VMEM((tm, D_C), F32),
                        pltpu.VMEM((tm, D_C), F32),
                        pltpu.VMEM((1, D_C), F32),
                        pltpu.VMEM((H_D, DK_D, DV_D), F32),
                        pltpu.VMEM((tm, D_MODEL), BF16)],
        compiler_params=pltpu.CompilerParams(dimension_semantics=("arbitrary", "arbitrary"),
                                             vmem_limit_bytes=VMEM_LIMIT),
        name="odd_mixer",
    )(x, h0, st0, buf8, *weights)


def _ffn_kernel(x_ref, buf0_ref, wup_ref, wdw_ref, bdw_ref, wdown_ref, lng_ref, lnb_ref,
                y_ref, bufnew_ref, hbuf_s, *, tm):
    t = pl.program_id(1)
    nt = pl.num_programs(1)

    @pl.when(t == 0)
    def _():
        hbuf_s[0:8, :] = buf0_ref[0]

    x = x_ref[0]
    xb = x.astype(BF16)
    acc = jnp.zeros((tm, D_MODEL), F32)
    for c0 in range(0, D_FF, FFN_COLS):
        cs = slice(c0, min(c0 + FFN_COLS, D_FF))
        if cs.stop <= FFN_MAIN:
            hg = _dot(xb, wup_ref[:, cs])
            hv = _dot(xb, wup_ref[:, FFN_MAIN + cs.start:FFN_MAIN + cs.stop])
        else:
            both = _dot(xb, wup_ref[:, 2 * FFN_MAIN:2 * D_FF])
            hg, hv = both[:, 0:D_FF - FFN_MAIN], both[:, D_FF - FFN_MAIN:]
        hbuf_s[8:8 + tm, cs] = hg
        cv = (bdw_ref[:, cs] + wdw_ref[0:1, cs] * hbuf_s[6:6 + tm, cs] + wdw_ref[1:2, cs] * hbuf_s[7:7 + tm, cs]
              + wdw_ref[2:3, cs] * hg)
        act = jax.nn.gelu(cv) * hv
        acc = acc + _dot(act.astype(BF16), wdown_ref[cs, :])
    hbuf_s[0:8, :] = hbuf_s[tm:tm + 8, :]
    y_ref[0] = _layernorm(ALPHA * x + acc, lng_ref[...], lnb_ref[...])

    @pl.when(t == nt - 1)
    def _():
        bufnew_ref[0] = hbuf_s[0:8, :]


def _ffn_layer(x, buf8, w, tm):
    bsz, L, _ = x.shape
    nt = L // tm
    full = lambda a: pl.BlockSpec(a.shape, lambda b, t: (0,) * a.ndim)
    weights = (w['wup'], w['wdw'], w['bdw'], w['wdown'], w['lng'], w['lnb'])
    return pl.pallas_call(
        functools.partial(_ffn_kernel, tm=tm),
        grid=(bsz, nt),
        in_specs=[pl.BlockSpec((1, tm, D_MODEL), lambda b, t: (b, t, 0)),
                  pl.BlockSpec((1, 8, D_FF), lambda b, t: (b, 0, 0))] + [full(a) for a in weights],
        out_specs=[pl.BlockSpec((1, tm, D_MODEL), lambda b, t: (b, t, 0)),
                   pl.BlockSpec((1, 8, D_FF), lambda b, t: (b, 0, 0))],
        out_shape=[jax.ShapeDtypeStruct((bsz, L, D_MODEL), F32),
                   jax.ShapeDtypeStruct((bsz, 8, D_FF), F32)],
        scratch_shapes=[pltpu.VMEM((tm + 8, D_FF), F32)],
        compiler_params=pltpu.CompilerParams(dimension_semantics=("arbitrary", "arbitrary"),
                                             vmem_limit_bytes=VMEM_LIMIT),
        name="conv_ffn",
    )(x, buf8, *weights)


def _row(v):
    return v.reshape(1, -1).astype(F32)


def _pad_rows(a, rows):
    return jnp.pad(a, ((0, rows - a.shape[0]), (0, 0)))


def _block_diag(w):
    h, d, _ = w.shape
    return jnp.einsum('hij,hg->higj', w, jnp.eye(h, dtype=w.dtype)).reshape(h * d, h * d)


def _prep_even(i, we_in, we_lr, be_lr, ge_gla, we_dw, be_dw, ge_cn, be_cn, we_out, ln1_g, ln1_b, l):
    w_in = we_in[i]
    lr0 = 2 * H_A * DK_A + 2 * H_A * DV_A
    win = jnp.concatenate([w_in[:, :lr0], w_in[:, lr0 + R_A:], w_in[:, lr0:lr0 + R_A],
                           jnp.zeros((D_MODEL, LANE - R_A), w_in.dtype)], axis=1).astype(BF16)
    return dict(win=win, wlr=_pad_rows(we_lr[i], LANE).astype(BF16), blr=_row(be_lr[i]), ggla=_row(ge_gla[i]),
                wdw=_pad_rows(we_dw[i], 32), bdw=_row(be_dw[i]), gcn=_row(ge_cn[i]), bcn=_row(be_cn[i]),
                wout=we_out[i].astype(BF16), lng=_row(ln1_g[l]), lnb=_row(ln1_b[l]),
                bdk=jnp.asarray(_block_mask(H_A * CHUNK, H_A * DK_A, CHUNK, DK_A), BF16),
                bdv=jnp.asarray(_block_mask(H_A * CHUNK, H_A * DV_A, CHUNK, DV_A), BF16),
                bdst=jnp.asarray(_block_mask(H_A * DV_A, H_A * DK_A, DV_A, DK_A), F32))


def _prep_odd(i, wo_in, wo_conv, bo_conv, wo_rg, bo_rg, wo_ig, bo_ig, lam_lru, a_log, dt_bias, go_delta, wo_out,
              ln1_g, ln1_b, l):
    w_in = wo_in[i]
    win = jnp.concatenate([w_in, jnp.zeros((D_MODEL, O_END - w_in.shape[1]), w_in.dtype)], axis=1).astype(BF16)
    head_row = lambda v: jnp.pad(v.astype(F32), (H_D, LANE - 2 * H_D)).reshape(1, LANE)
    return dict(win=win, wcv=_pad_rows(wo_conv[i], 8), bcv=_row(bo_conv[i]),
                wg=jnp.concatenate([_block_diag(wo_rg[i]), _block_diag(wo_ig[i])], axis=1).astype(BF16),
                bg=_row(jnp.concatenate([bo_rg[i], bo_ig[i]])), lam=_row(lam_lru[i]),
                alog=head_row(a_log[i]), dtb=head_row(dt_bias[i]), gdl=_row(go_delta[i]),
                wout=wo_out[i].astype(BF16), lng=_row(ln1_g[l]), lnb=_row(ln1_b[l]),
                bd=jnp.asarray(_block_mask(H_D * CHUNK, H_D * CHUNK, CHUNK, CHUNK), BF16),
                bdk=jnp.asarray(_block_mask(H_D * CHUNK, H_D * DK_D, CHUNK, DK_D), BF16))


def _prep_ffn(l, w_up, w_fdw, b_fdw, w_down, ln2_g, ln2_b):
    wg, wv = w_up[l][:, :D_FF], w_up[l][:, D_FF:]
    wup = jnp.concatenate([wg[:, :FFN_MAIN], wv[:, :FFN_MAIN], wg[:, FFN_MAIN:], wv[:, FFN_MAIN:]], axis=1)
    return dict(wup=wup.astype(BF16), wdw=_pad_rows(w_fdw[l], 8), bdw=_row(b_fdw[l]),
                wdown=w_down[l].astype(BF16), lng=_row(ln2_g[l]), lnb=_row(ln2_b[l]))


def _front_pad(buf, rows):
    return jnp.pad(buf, ((0, 0), (rows - buf.shape[1], 0), (0, 0)))


def _gla_state_to_blockdiag(s):
    s_t = jnp.swapaxes(s, 2, 3)
    return jnp.concatenate([jnp.pad(s_t[:, h], ((0, 0), (0, 0), (h * DK_A, (H_A - 1 - h) * DK_A)))
                            for h in range(H_A)], axis=1)


def _gla_state_from_blockdiag(s_bd):
    bsz = s_bd.shape[0]
    s5 = s_bd.reshape(bsz, H_A, DV_A, H_A, DK_A)
    return jnp.stack([jnp.swapaxes(s5[:, h, :, h, :], 1, 2) for h in range(H_A)], axis=1)


def _trunk(x, states, mix_w, ffn_w):
    L = x.shape[1]
    new_states = []
    for l in range(DEPTH):
        st = states[l]
        if l % 2 == 0:
            x, s_bd, buf = _even_layer(x, _gla_state_to_blockdiag(st[0]), _front_pad(st[1], 32), mix_w[l],
                                       min(L, EVEN_TILE))
            mix_new = (_gla_state_from_blockdiag(s_bd), buf[:, 32 - (W_B - 1):])
        else:
            x, h, s, buf = _odd_layer(x, st[0][:, None, :], st[1], _front_pad(st[2], 8), mix_w[l], min(L, ODD_TILE))
            mix_new = (h[:, 0], s, buf[:, 8 - (W_S - 1):])
        x, fbuf = _ffn_layer(x, _front_pad(st[-1], 8), ffn_w[l], min(L, FFN_TILE))
        new_states.append((*mix_new, fbuf[:, 8 - (W_F - 1):]))
    return x, new_states


def _zero_states(bsz):
    z = lambda *s: jnp.zeros((bsz,) + s, F32)
    return [(z(H_A, DK_A, DV_A), z(W_B - 1, D_B), z(W_F - 1, D_FF)) if l % 2 == 0 else
            (z(D_C), z(H_D, DK_D, DV_D), z(W_S - 1, CONV_ODD), z(W_F - 1, D_FF)) for l in range(DEPTH)]


def kernel(x_prompt, x_sample, state_l0_gla, cache_l0_dwconv, cache_l0_ffn, state_l1_lru, state_l1_delta, cache_l1_conv, cache_l1_ffn, state_l2_gla, cache_l2_dwconv, cache_l2_ffn, state_l3_lru, state_l3_delta, cache_l3_conv, cache_l3_ffn, we_in, we_lr, be_lr, ge_gla, we_dw, be_dw, ge_cn, be_cn, we_out, wo_in, wo_conv, bo_conv, wo_rg, bo_rg, wo_ig, bo_ig, lam_lru, a_log, dt_bias, go_delta, wo_out, w_up, w_fdw, b_fdw, w_down, ln1_g, ln1_b, ln2_g, ln2_b):
    mix_w = []
    for l in range(DEPTH):
        if l % 2 == 0:
            mix_w.append(_prep_even(l // 2, we_in, we_lr, be_lr, ge_gla, we_dw, be_dw, ge_cn, be_cn, we_out,
                                    ln1_g, ln1_b, l))
        else:
            mix_w.append(_prep_odd(l // 2, wo_in, wo_conv, bo_conv, wo_rg, bo_rg, wo_ig, bo_ig, lam_lru, a_log,
                                   dt_bias, go_delta, wo_out, ln1_g, ln1_b, l))
    ffn_w = [_prep_ffn(l, w_up, w_fdw, b_fdw, w_down, ln2_g, ln2_b) for l in range(DEPTH)]
    y_prompt, new_p = _trunk(x_prompt, _zero_states(x_prompt.shape[0]), mix_w, ffn_w)
    sample_states = [(state_l0_gla, cache_l0_dwconv, cache_l0_ffn),
                     (state_l1_lru, state_l1_delta, cache_l1_conv, cache_l1_ffn),
                     (state_l2_gla, cache_l2_dwconv, cache_l2_ffn),
                     (state_l3_lru, state_l3_delta, cache_l3_conv, cache_l3_ffn)]
    y_sample, new_s = _trunk(x_sample, sample_states, mix_w, ffn_w)
    flat = lambda ns: [a for layer in ns for a in layer]
    return (y_prompt, y_sample, *flat(new_p), *flat(new_s))
```
